```python
import jax, jax.numpy as jnp
from jax import lax
import numpy as np

D_MODEL = 1024
BATCH = 2
SEQ = 8192
DEPTH = 1

CHUNK = 64
N_LEFT_CHUNKS = 8
BAND = (N_LEFT_CHUNKS + 1) * CHUNK
ATT_HEADS = 8
ATT_HEAD_DIM = 64
D_ATT = ATT_HEADS * ATT_HEAD_DIM
D_CONV = D_MODEL // 2
CONV_WIDTH = 3
MAX_REL = 128
N_REL = 2 * MAX_REL + 1
EPS = 1e-6
NEG_BIG = -1e30
IN_SIZES = (D_ATT, D_ATT, D_ATT, D_ATT, D_CONV, D_CONV, D_CONV, D_CONV, D_MODEL, D_MODEL)
IN_COLS = sum(IN_SIZES)
IN_SPLITS = tuple(int(s) for s in np.cumsum(IN_SIZES)[:-1])

kernel_name = "hybrid_chunk_attn_shortconv_gated_block"


def rmsnorm(x, g):
    xf = x.astype(jnp.float32)
    r = lax.rsqrt(jnp.mean(xf * xf, axis=-1, keepdims=True) + EPS)
    return (xf * r).astype(x.dtype) * g


def chunked_rel_attention(q, k, v, rel_bias):
    b, s, h, dh = q.shape
    nc = s // CHUNK
    qc = q.reshape(b, nc, CHUNK, h, dh)
    kc = k.reshape(b, nc, CHUNK, h, dh)
    vc = v.reshape(b, nc, CHUNK, h, dh)
    pad = ((0, 0), (N_LEFT_CHUNKS, 0), (0, 0), (0, 0), (0, 0))
    kp = jnp.pad(kc, pad)
    vp = jnp.pad(vc, pad)
    kb = jnp.concatenate([kp[:, j:j + nc] for j in range(N_LEFT_CHUNKS + 1)], axis=2)
    vb = jnp.concatenate([vp[:, j:j + nc] for j in range(N_LEFT_CHUNKS + 1)], axis=2)
    scale = ATT_HEAD_DIM ** -0.5
    scores = jnp.einsum('bcqhd,bckhd->bhcqk', qc, kb).astype(jnp.float32) * scale
    rel = np.arange(CHUNK)[:, None] + N_LEFT_CHUNKS * CHUNK - np.arange(BAND)[None, :]
    idx = np.clip(rel, -MAX_REL, MAX_REL) + MAX_REL
    bias = rel_bias[:, idx].astype(jnp.float32)
    kpos = np.arange(nc)[:, None] * CHUNK - N_LEFT_CHUNKS * CHUNK + np.arange(BAND)[None, :]
    valid = jnp.asarray(kpos >= 0)
    scores = jnp.where(valid[None, None, :, None, :], scores + bias[None, :, None], NEG_BIG)
    p = jax.nn.softmax(scores, axis=-1).astype(v.dtype)
    out = jnp.einsum('bhcqk,bckhd->bcqhd', p, vb)
    return out.reshape(b, s, h * dh)


def causal_depthwise_conv(u, w, bias):
    s = u.shape[1]
    up = jnp.pad(u, ((0, 0), (CONV_WIDTH - 1, 0), (0, 0)))
    return sum(w[t] * up[:, t:t + s] for t in range(CONV_WIDTH)) + bias


def setup_inputs(seed: int = 0) -> dict:
    key = jax.random.key(seed)
    ks = jax.random.split(key, 12)
    f32 = jnp.float32
    x = jax.random.normal(ks[0], (BATCH, SEQ, D_MODEL), f32)
    norm_g = 1.0 + 0.05 * jax.random.normal(ks[1], (DEPTH, D_MODEL), f32)
    w_in = jax.random.normal(ks[2], (DEPTH, D_MODEL, IN_COLS), f32) * D_MODEL ** -0.5
    rel_bias = 0.2 * jax.random.normal(ks[3], (DEPTH, ATT_HEADS, N_REL), f32)
    w_att_out = jax.random.normal(ks[4], (DEPTH, D_ATT, D_MODEL), f32) * D_ATT ** -0.5
    conv_w = jax.random.normal(ks[5], (DEPTH, CONV_WIDTH, D_CONV), f32) * CONV_WIDTH ** -0.5
    conv_b = 0.02 * jax.random.normal(ks[6], (DEPTH, D_CONV), f32)
    w_conv_out = jax.random.normal(ks[7], (DEPTH, D_CONV, D_MODEL), f32) * D_CONV ** -0.5
    w_out = jax.random.normal(ks[8], (DEPTH, D_MODEL, D_MODEL), f32) * D_MODEL ** -0.5
    final_norm_g = 1.0 + 0.05 * jax.random.normal(ks[9], (D_MODEL,), f32)
    return {"x": x, "norm_g": norm_g, "w_in": w_in, "rel_bias": rel_bias,
            "w_att_out": w_att_out, "conv_w": conv_w, "conv_b": conv_b,
            "w_conv_out": w_conv_out, "w_out": w_out, "final_norm_g": final_norm_g}


def reference(x, norm_g, w_in, rel_bias, w_att_out, conv_w, conv_b, w_conv_out, w_out, final_norm_g):
    b, s, _ = x.shape
    for l in range(DEPTH):
        h = rmsnorm(x, norm_g[l])
        proj = jnp.einsum('bsd,de->bse', h, w_in[l])
        q, k, v, z_att, gb, gc, u, z_conv, g_att, g_conv = jnp.split(proj, IN_SPLITS, axis=-1)
        att = chunked_rel_attention(q.reshape(b, s, ATT_HEADS, ATT_HEAD_DIM),
                                    k.reshape(b, s, ATT_HEADS, ATT_HEAD_DIM),
                                    v.reshape(b, s, ATT_HEADS, ATT_HEAD_DIM),
                                    rel_bias[l])
        y_att = jnp.einsum('bsc,cd->bsd', att * jax.nn.silu(z_att), w_att_out[l])
        vconv = causal_depthwise_conv(gc * u, conv_w[l], conv_b[l])
        y_conv = jnp.einsum('bsc,cd->bsd', gb * vconv * jax.nn.silu(z_conv), w_conv_out[l])
        m = jax.nn.sigmoid(g_att) * y_att + jax.nn.sigmoid(g_conv) * y_conv
        x = x + jnp.einsum('bsd,de->bse', m, w_out[l])
    return rmsnorm(x, final_norm_g)
```

```python
import functools

import numpy as np
import jax
import jax.numpy as jnp
from jax import lax
from jax.experimental import pallas as pl
from jax.experimental.pallas import tpu as pltpu

D_MODEL = 1024
CHUNK = 64
N_LEFT_CHUNKS = 8
ATT_HEADS = 8
ATT_HEAD_DIM = 64
D_ATT = ATT_HEADS * ATT_HEAD_DIM
D_CONV = D_MODEL // 2
CONV_WIDTH = 3
MAX_REL = 128
N_REL = 2 * MAX_REL + 1
EPS = 1e-6
NEG_BIG = -1e30
IN_COLS = 4 * D_ATT + 4 * D_CONV + 2 * D_MODEL

LANES = 128
HEADS_PER_LANE_GROUP = LANES // ATT_HEAD_DIM
N_PAIRS = D_ATT // LANES
KEY_TILE = 256
T_BLOCK = 512
Q_BLOCK = 256
HALO = N_LEFT_CHUNKS * CHUNK
WIN = HALO + Q_BLOCK
CARRY_ROWS = 8
VMEM_LIMIT_BYTES = 58 * 1024 * 1024

assert T_BLOCK == HALO and T_BLOCK % Q_BLOCK == 0 and HALO % KEY_TILE == 0

_C_Q, _C_K, _C_V, _C_ZA = 0, D_ATT, 2 * D_ATT, 3 * D_ATT
_C_GB = 4 * D_ATT
_C_GC, _C_U, _C_ZC = _C_GB + D_CONV, _C_GB + 2 * D_CONV, _C_GB + 3 * D_CONV
_C_GA = _C_GB + 4 * D_CONV
_C_GCV = _C_GA + D_MODEL


def _bias_index_and_mask():
    r = np.arange(Q_BLOCK)[:, None]
    c = np.arange(WIN)[None, :]
    rel = r + HALO - c
    idx = np.clip(rel, -MAX_REL, MAX_REL) + MAX_REL
    dc = (r // CHUNK + N_LEFT_CHUNKS) - c // CHUNK
    valid = (dc >= 0) & (dc <= N_LEFT_CHUNKS)
    return idx.astype(np.int32), valid


def _dot(a, b):
    return jnp.dot(a, b, preferred_element_type=jnp.float32)


def _block_kernel(x_ref, ng_ref, win_ref, tbl_ref, wao_ref, cw_ref, cb_ref, wco_ref, wo_ref, fg_ref,
                  out_ref, qm_ref, k_ref, v_ref, sz_ref, sga_ref, ycg_ref, att_ref, cu_ref,
                  *, steps_per_seq):
    i = pl.program_id(0)
    is_start = (i % steps_per_seq) == 0
    bf16 = jnp.bfloat16

    x = x_ref[...]
    r = lax.rsqrt(jnp.mean(x * x, axis=-1, keepdims=True) + EPS)
    h = ((x * r) * ng_ref[...]).astype(bf16)

    @pl.when(jnp.logical_not(is_start))
    def _():
        k_ref[:, 0:HALO, :] = k_ref[:, T_BLOCK:T_BLOCK + HALO, :]
        v_ref[:, 0:HALO, :] = v_ref[:, T_BLOCK:T_BLOCK + HALO, :]

    @pl.when(is_start)
    def _():
        cu_ref[0:CARRY_ROWS, :] = jnp.zeros((CARRY_ROWS, D_CONV), jnp.float32)

    lane = lax.broadcasted_iota(jnp.int32, (1, LANES), 1)
    low_half = lane < ATT_HEAD_DIM

    qkv = _dot(h, win_ref[:, _C_Q:_C_ZA])
    scale = ATT_HEAD_DIM ** -0.5
    for p in range(N_PAIRS):
        qp = (qkv[:, _C_Q + p * LANES:_C_Q + (p + 1) * LANES] * scale).astype(bf16)
        zero = jnp.zeros_like(qp)
        qm_ref[2 * p] = jnp.where(low_half, qp, zero)
        qm_ref[2 * p + 1] = jnp.where(low_half, zero, qp)
        k_ref[p, HALO:HALO + T_BLOCK, :] = qkv[:, _C_K + p * LANES:_C_K + (p + 1) * LANES].astype(bf16)
        v_ref[p, HALO:HALO + T_BLOCK, :] = qkv[:, _C_V + p * LANES:_C_V + (p + 1) * LANES].astype(bf16)

    sz_ref[...] = jax.nn.silu(_dot(h, win_ref[:, _C_ZA:_C_GB]))

    cp = _dot(h, win_ref[:, _C_GB:_C_GA])
    gb = cp[:, 0:D_CONV]
    cu = cp[:, D_CONV:2 * D_CONV] * cp[:, 2 * D_CONV:3 * D_CONV]
    zc = cp[:, 3 * D_CONV:4 * D_CONV]
    cu_ref[CARRY_ROWS:CARRY_ROWS + T_BLOCK, :] = cu
    cw = cw_ref[...]
    vconv = (cw[0:1, :] * cu_ref[CARRY_ROWS - 2:CARRY_ROWS - 2 + T_BLOCK, :]
             + cw[1:2, :] * cu_ref[CARRY_ROWS - 1:CARRY_ROWS - 1 + T_BLOCK, :]
             + cw[2:3, :] * cu) + cb_ref[...]
    cu_ref[0:CARRY_ROWS, :] = cu_ref[T_BLOCK:T_BLOCK + CARRY_ROWS, :]
    gconv = (gb * vconv * jax.nn.silu(zc)).astype(bf16)
    y_conv = _dot(gconv, wco_ref[...])
    ycg_ref[...] = jax.nn.sigmoid(_dot(h, win_ref[:, _C_GCV:IN_COLS])) * y_conv
    sga_ref[...] = jax.nn.sigmoid(_dot(h, win_ref[:, _C_GA:_C_GCV]))

    def attend(b, skip):
        q0 = b * Q_BLOCK
        k0 = q0 + skip * KEY_TILE
        k1 = q0 + WIN
        for p in range(N_PAIRS):
            kp = k_ref[p, k0:k1, :]
            vp = v_ref[p, k0:k1, :]
            halves = []
            for j in range(HEADS_PER_LANE_GROUP):
                hd = HEADS_PER_LANE_GROUP * p + j
                s = lax.dot_general(qm_ref[hd, q0:q0 + Q_BLOCK, :], kp, (((1,), (1,)), ((), ())),
                                    preferred_element_type=jnp.float32)
                s = s + tbl_ref[hd, :, skip * KEY_TILE:WIN]
                m = jnp.max(s, axis=-1, keepdims=True)
                e = jnp.exp(s - m)
                l = jnp.sum(e, axis=-1, keepdims=True)
                o = _dot(e.astype(bf16), vp)
                halves.append(o * (1.0 / l))
            att_ref[q0:q0 + Q_BLOCK, p * LANES:(p + 1) * LANES] = jnp.where(low_half, halves[0], halves[1])

    n_qb = T_BLOCK // Q_BLOCK

    @pl.when(is_start)
    def _():
        for b in range(n_qb):
            attend(b, (HALO - b * Q_BLOCK) // KEY_TILE)

    @pl.when(jnp.logical_not(is_start))
    def _():
        for b in range(n_qb):
            attend(b, 0)

    gatt = (att_ref[...] * sz_ref[...]).astype(bf16)
    y_att = _dot(gatt, wao_ref[...])
    m = (sga_ref[...] * y_att + ycg_ref[...]).astype(bf16)
    xo = x_ref[...] + _dot(m, wo_ref[...])
    r2 = lax.rsqrt(jnp.mean(xo * xo, axis=-1, keepdims=True) + EPS)
    out_ref[...] = (xo * r2) * fg_ref[...]


def _const_spec(shape):
    zeros = (0,) * len(shape)
    return pl.BlockSpec(shape, lambda i: zeros, pipeline_mode=pl.Buffered(1))


def kernel(x, norm_g, w_in, rel_bias, w_att_out, conv_w, conv_b, w_conv_out, w_out, final_norm_g):
    b, s, d = x.shape
    assert d == D_MODEL and s % T_BLOCK == 0 and norm_g.shape[0] == 1
    bf16 = jnp.bfloat16
    n_tok = b * s
    idx, valid = _bias_index_and_mask()
    tbl = jnp.where(jnp.asarray(valid)[None], rel_bias[0][:, idx], NEG_BIG).astype(jnp.float32)

    operands = (
        x.reshape(n_tok, d), norm_g[0][None, :], w_in[0].astype(bf16), tbl,
        w_att_out[0].astype(bf16), conv_w[0], conv_b[0][None, :], w_conv_out[0].astype(bf16),
        w_out[0].astype(bf16), final_norm_g[None, :],
    )
    in_specs = [pl.BlockSpec((T_BLOCK, d), lambda i: (i, 0))] + [_const_spec(o.shape) for o in operands[1:]]
    out = pl.pallas_call(
        functools.partial(_block_kernel, steps_per_seq=s // T_BLOCK),
        grid=(n_tok // T_BLOCK,),
        in_specs=in_specs,
        out_specs=pl.BlockSpec((T_BLOCK, d), lambda i: (i, 0)),
        out_shape=jax.ShapeDtypeStruct((n_tok, d), jnp.float32),
        scratch_shapes=[
            pltpu.VMEM((ATT_HEADS, T_BLOCK, LANES), bf16),
            pltpu.VMEM((N_PAIRS, HALO + T_BLOCK, LANES), bf16),
            pltpu.VMEM((N_PAIRS, HALO + T_BLOCK, LANES), bf16),
            pltpu.VMEM((T_BLOCK, D_ATT), jnp.float32),
            pltpu.VMEM((T_BLOCK, D_MODEL), jnp.float32),
            pltpu.VMEM((T_BLOCK, D_MODEL), jnp.float32),
            pltpu.VMEM((T_BLOCK, D_ATT), jnp.float32),
            pltpu.VMEM((CARRY_ROWS + T_BLOCK, D_CONV), jnp.float32),
        ],
        compiler_params=pltpu.CompilerParams(
            dimension_semantics=("arbitrary",), vmem_limit_bytes=VMEM_LIMIT_BYTES),
        name="hybrid_block",
    )(*operands)
    return out.reshape(b, s, d)
```

```python
import functools

import numpy as np
import jax
import jax.numpy as jnp
from jax import lax
from jax.experimental import pallas as pl
from jax.experimental.pallas import tpu as pltpu

D_MODEL = 1024
CHUNK = 64
N_LEFT_CHUNKS = 8
ATT_HEADS = 8
ATT_HEAD_DIM = 64
D_ATT = ATT_HEADS * ATT_HEAD_DIM
D_CONV = D_MODEL // 2
CONV_WIDTH = 3
MAX_REL = 128
N_REL = 2 * MAX_REL + 1
EPS = 1e-6
NEG_BIG = -1e30
IN_COLS = 4 * D_ATT + 4 * D_CONV + 2 * D_MODEL

LANES = 128
HEADS_PER_LANE_GROUP = LANES // ATT_HEAD_DIM
N_PAIRS = D_ATT // LANES
KEY_TILE = 256
T_BLOCK = 512
Q_BLOCK = 256
HALO = N_LEFT_CHUNKS * CHUNK
WIN = HALO + Q_BLOCK
CARRY_ROWS = 8
CHUNK_SHIFT = CHUNK.bit_length() - 1
REL_PAD = -(-N_REL // LANES) * LANES
VMEM_LIMIT_BYTES = 58 * 1024 * 1024

assert T_BLOCK == HALO and T_BLOCK % Q_BLOCK == 0 and HALO % KEY_TILE == 0

_C_Q, _C_K, _C_V, _C_ZA = 0, D_ATT, 2 * D_ATT, 3 * D_ATT
_C_GB = 4 * D_ATT
_C_GC, _C_U, _C_ZC = _C_GB + D_CONV, _C_GB + 2 * D_CONV, _C_GB + 3 * D_CONV
_C_GA = _C_GB + 4 * D_CONV
_C_GCV = _C_GA + D_MODEL


def _dot(a, b):
    return jnp.dot(a, b, preferred_element_type=jnp.float32)


def _build_bias_table(rb_ref, tbl_ref):
    f32, bf16 = jnp.float32, jnp.bfloat16
    rb = rb_ref[...]
    hi = rb.astype(bf16)
    rem = rb - hi.astype(f32)
    mid = rem.astype(bf16)
    lo = (rem - mid.astype(f32)).astype(bf16)
    src = lax.broadcasted_iota(jnp.int32, (REL_PAD, WIN), 0)
    m = lax.broadcasted_iota(jnp.int32, (REL_PAD, WIN), 1)
    idx = jnp.where(m > HALO + MAX_REL, 2 * MAX_REL, jnp.clip(HALO + MAX_REL - m, 0, 2 * MAX_REL))
    onehot = (src == idx).astype(bf16)
    g = _dot(hi, onehot) + _dot(mid, onehot) + _dot(lo, onehot)
    r = lax.broadcasted_iota(jnp.int32, (Q_BLOCK, WIN), 0)
    c = lax.broadcasted_iota(jnp.int32, (Q_BLOCK, WIN), 1)
    dc = (lax.shift_right_logical(r, CHUNK_SHIFT) + N_LEFT_CHUNKS) - lax.shift_right_logical(c, CHUNK_SHIFT)
    valid = jnp.logical_and(dc >= 0, dc <= N_LEFT_CHUNKS)
    for hd in range(ATT_HEADS):
        rows = jnp.broadcast_to(g[hd:hd + 1, :], (Q_BLOCK, WIN))
        toeplitz = pltpu.roll(rows, 0, 1, stride=1, stride_axis=0)
        tbl_ref[hd] = jnp.where(valid, toeplitz, NEG_BIG)


def _block_kernel(x_ref, ng_ref, win_ref, rb_ref, wao_ref, cw_ref, cb_ref, wco_ref, wo_ref, fg_ref,
                  out_ref, tbl_ref, qm_ref, k_ref, v_ref, sz_ref, sga_ref, ycg_ref, att_ref, cu_ref,
                  *, steps_per_seq):
    i = pl.program_id(0)
    is_start = (i % steps_per_seq) == 0
    bf16 = jnp.bfloat16

    @pl.when(i == 0)
    def _():
        _build_bias_table(rb_ref, tbl_ref)

    x = x_ref[...]
    r = lax.rsqrt(jnp.mean(x * x, axis=-1, keepdims=True) + EPS)
    h = ((x * r) * ng_ref[...]).astype(bf16)

    @pl.when(jnp.logical_not(is_start))
    def _():
        k_ref[:, 0:HALO, :] = k_ref[:, T_BLOCK:T_BLOCK + HALO, :]
        v_ref[:, 0:HALO, :] = v_ref[:, T_BLOCK:T_BLOCK + HALO, :]

    @pl.when(is_start)
    def _():
        cu_ref[0:CARRY_ROWS, :] = jnp.zeros((CARRY_ROWS, D_CONV), jnp.float32)

    lane = lax.broadcasted_iota(jnp.int32, (1, LANES), 1)
    low_half = lane < ATT_HEAD_DIM

    qkv = _dot(h, win_ref[:, _C_Q:_C_ZA])
    scale = ATT_HEAD_DIM ** -0.5
    for p in range(N_PAIRS):
        qp = (qkv[:, _C_Q + p * LANES:_C_Q + (p + 1) * LANES] * scale).astype(bf16)
        zero = jnp.zeros_like(qp)
        qm_ref[2 * p] = jnp.where(low_half, qp, zero)
        qm_ref[2 * p + 1] = jnp.where(low_half, zero, qp)
        k_ref[p, HALO:HALO + T_BLOCK, :] = qkv[:, _C_K + p * LANES:_C_K + (p + 1) * LANES].astype(bf16)
        v_ref[p, HALO:HALO + T_BLOCK, :] = qkv[:, _C_V + p * LANES:_C_V + (p + 1) * LANES].astype(bf16)

    sz_ref[...] = jax.nn.silu(_dot(h, win_ref[:, _C_ZA:_C_GB]))

    cp = _dot(h, win_ref[:, _C_GB:_C_GA])
    gb = cp[:, 0:D_CONV]
    cu = cp[:, D_CONV:2 * D_CONV] * cp[:, 2 * D_CONV:3 * D_CONV]
    zc = cp[:, 3 * D_CONV:4 * D_CONV]
    cu_ref[CARRY_ROWS:CARRY_ROWS + T_BLOCK, :] = cu
    cw = cw_ref[...]
    vconv = (cw[0:1, :] * cu_ref[CARRY_ROWS - 2:CARRY_ROWS - 2 + T_BLOCK, :]
             + cw[1:2, :] * cu_ref[CARRY_ROWS - 1:CARRY_ROWS - 1 + T_BLOCK, :]
             + cw[2:3, :] * cu) + cb_ref[...]
    cu_ref[0:CARRY_ROWS, :] = cu_ref[T_BLOCK:T_BLOCK + CARRY_ROWS, :]
    gconv = (gb * vconv * jax.nn.silu(zc)).astype(bf16)
    y_conv = _dot(gconv, wco_ref[...])
    ycg_ref[...] = jax.nn.sigmoid(_dot(h, win_ref[:, _C_GCV:IN_COLS])) * y_conv
    sga_ref[...] = jax.nn.sigmoid(_dot(h, win_ref[:, _C_GA:_C_GCV]))

    def attend(b, skip):
        q0 = b * Q_BLOCK
        k0 = q0 + skip * KEY_TILE
        k1 = q0 + WIN
        for p in range(N_PAIRS):
            kp = k_ref[p, k0:k1, :]
            vp = v_ref[p, k0:k1, :]
            halves = []
            for j in range(HEADS_PER_LANE_GROUP):
                hd = HEADS_PER_LANE_GROUP * p + j
                s = lax.dot_general(qm_ref[hd, q0:q0 + Q_BLOCK, :], kp, (((1,), (1,)), ((), ())),
                                    preferred_element_type=jnp.float32)
                s = s + tbl_ref[hd, :, skip * KEY_TILE:WIN]
                m = jnp.max(s, axis=-1, keepdims=True)
                e = jnp.exp(s - m)
                l = jnp.sum(e, axis=-1, keepdims=True)
                o = _dot(e.astype(bf16), vp)
                halves.append(o * (1.0 / l))
            att_ref[q0:q0 + Q_BLOCK, p * LANES:(p + 1) * LANES] = jnp.where(low_half, halves[0], halves[1])

    n_qb = T_BLOCK // Q_BLOCK

    @pl.when(is_start)
    def _():
        for b in range(n_qb):
            attend(b, (HALO - b * Q_BLOCK) // KEY_TILE)

    @pl.when(jnp.logical_not(is_start))
    def _():
        for b in range(n_qb):
            attend(b, 0)

    gatt = (att_ref[...] * sz_ref[...]).astype(bf16)
    y_att = _dot(gatt, wao_ref[...])
    m = (sga_ref[...] * y_att + ycg_ref[...]).astype(bf16)
    xo = x_ref[...] + _dot(m, wo_ref[...])
    r2 = lax.rsqrt(jnp.mean(xo * xo, axis=-1, keepdims=True) + EPS)
    out_ref[...] = (xo * r2) * fg_ref[...]


def _const_spec(shape):
    zeros = (0,) * len(shape)
    return pl.BlockSpec(shape, lambda i: zeros, pipeline_mode=pl.Buffered(1))


def kernel(x, norm_g, w_in, rel_bias, w_att_out, conv_w, conv_b, w_conv_out, w_out, final_norm_g):
    b, s, d = x.shape
    assert d == D_MODEL and s % T_BLOCK == 0 and norm_g.shape[0] == 1
    bf16 = jnp.bfloat16
    n_tok = b * s
    rb_padded = jnp.pad(rel_bias[0], ((0, 0), (0, REL_PAD - N_REL)))

    operands = (
        x.reshape(n_tok, d), norm_g[0][None, :], w_in[0].astype(bf16), rb_padded,
        w_att_out[0].astype(bf16), conv_w[0], conv_b[0][None, :], w_conv_out[0].astype(bf16),
        w_out[0].astype(bf16), final_norm_g[None, :],
    )
    in_specs = [pl.BlockSpec((T_BLOCK, d), lambda i: (i, 0))] + [_const_spec(o.shape) for o in operands[1:]]
    out = pl.pallas_call(
        functools.partial(_block_kernel, steps_per_seq=s // T_BLOCK),
        grid=(n_tok // T_BLOCK,),
        in_specs=in_specs,
        out_specs=pl.BlockSpec((T_BLOCK, d), lambda i: (i, 0)),
        out_shape=jax.ShapeDtypeStruct((n_tok, d), jnp.float32),
        scratch_shapes=[
            pltpu.VMEM((ATT_HEADS, Q_BLOCK, WIN), jnp.float32),
            pltpu.VMEM((ATT_HEADS, T_BLOCK, LANES), bf16),
            pltpu.VMEM((N_PAIRS, HALO + T_BLOCK, LANES), bf16),
            pltpu.VMEM((N_PAIRS, HALO + T_BLOCK, LANES), bf16),
            pltpu.VMEM((T_BLOCK, D_ATT), jnp.float32),
            pltpu.VMEM((T_BLOCK, D_MODEL), jnp.float32),
            pltpu.VMEM((T_BLOCK, D_MODEL), jnp.float32),
            pltpu.VMEM((T_BLOCK, D_ATT), jnp.float32),
            pltpu.VMEM((CARRY_ROWS + T_BLOCK, D_CONV), jnp.float32),
        ],
        compiler_params=pltpu.CompilerParams(
            dimension_semantics=("arbitrary",), vmem_limit_bytes=VMEM_LIMIT_BYTES),
        name="hybrid_block",
    )(*operands)
    return out.reshape(b, s, d)
```

```python
import functools

import jax
import jax.numpy as jnp
from jax import lax
from jax.experimental import pallas as pl
from jax.experimental.pallas import tpu as pltpu

D_MODEL = 1024
CHUNK = 64
N_LEFT_CHUNKS = 8
ATT_HEADS = 8
ATT_HEAD_DIM = 64
D_ATT = ATT_HEADS * ATT_HEAD_DIM
D_CONV = D_MODEL // 2
CONV_WIDTH = 3
MAX_REL = 128
N_REL = 2 * MAX_REL + 1
EPS = 1e-6
NEG_BIG = -1e30
IN_COLS = 4 * D_ATT + 4 * D_CONV + 2 * D_MODEL

LANES = 128
HEADS_PER_LANE_GROUP = LANES // ATT_HEAD_DIM
N_PAIRS = D_ATT // LANES
MXU_TILE = 256
T_BLOCK = 512
Q_BLOCK = 256
HALO = N_LEFT_CHUNKS * CHUNK
WIN = HALO + Q_BLOCK
CARRY_ROWS = 8
CHUNK_SHIFT = CHUNK.bit_length() - 1
REL_PAD = -(-N_REL // LANES) * LANES
VMEM_LIMIT_BYTES = 58 * 1024 * 1024

assert T_BLOCK == HALO and T_BLOCK % Q_BLOCK == 0 and HEADS_PER_LANE_GROUP == 2

_C_Q, _C_K, _C_V, _C_ZA = 0, D_ATT, 2 * D_ATT, 3 * D_ATT
_C_GB = 4 * D_ATT
_C_GC, _C_U, _C_ZC = _C_GB + D_CONV, _C_GB + 2 * D_CONV, _C_GB + 3 * D_CONV
_C_GA = _C_GB + 4 * D_CONV
_C_GCV = _C_GA + D_MODEL


def _dot(a, b):
    return jnp.dot(a, b, preferred_element_type=jnp.float32)


def _build_bias_table(rb_ref, tbl_ref):
    f32, bf16 = jnp.float32, jnp.bfloat16
    rb = rb_ref[...]
    hi = rb.astype(bf16)
    rem = rb - hi.astype(f32)
    mid = rem.astype(bf16)
    lo = (rem - mid.astype(f32)).astype(bf16)
    src = lax.broadcasted_iota(jnp.int32, (REL_PAD, WIN), 0)
    m = lax.broadcasted_iota(jnp.int32, (REL_PAD, WIN), 1)
    idx = jnp.where(m > HALO + MAX_REL, 2 * MAX_REL, jnp.clip(HALO + MAX_REL - m, 0, 2 * MAX_REL))
    onehot = (src == idx).astype(bf16)
    g = _dot(hi, onehot) + _dot(mid, onehot) + _dot(lo, onehot)
    r = lax.broadcasted_iota(jnp.int32, (Q_BLOCK, WIN), 0)
    c = lax.broadcasted_iota(jnp.int32, (Q_BLOCK, WIN), 1)
    dc = (lax.shift_right_logical(r, CHUNK_SHIFT) + N_LEFT_CHUNKS) - lax.shift_right_logical(c, CHUNK_SHIFT)
    valid = jnp.logical_and(dc >= 0, dc <= N_LEFT_CHUNKS)
    for hd in range(ATT_HEADS):
        rows = jnp.broadcast_to(g[hd:hd + 1, :], (Q_BLOCK, WIN))
        toeplitz = pltpu.roll(rows, 0, 1, stride=1, stride_axis=0)
        tbl_ref[hd] = jnp.where(valid, toeplitz, NEG_BIG)


def _block_kernel(x_ref, ng_ref, win_ref, rb_ref, wao_ref, cw_ref, cb_ref, wco_ref, wo_ref, fg_ref,
                  out_ref, tbl_ref, h_ref, qx_ref, kh_ref, v_ref, sz_ref, sga_ref, ycg_ref, att_ref,
                  cu_ref, gconv_ref, *, steps_per_seq):
    i = pl.program_id(0)
    is_start = (i % steps_per_seq) == 0
    f32, bf16 = jnp.float32, jnp.bfloat16

    lane = lax.broadcasted_iota(jnp.int32, (1, LANES), 1)
    own_half = [(lane >= j * ATT_HEAD_DIM) & (lane < (j + 1) * ATT_HEAD_DIM) for j in range(2)]
    offset_lane = [lane == (1 - j) * ATT_HEAD_DIM for j in range(2)]

    @pl.when(i == 0)
    def _():
        _build_bias_table(rb_ref, tbl_ref)

    @pl.when(is_start)
    def _():
        for hd in range(ATT_HEADS):
            row = jnp.where(offset_lane[hd % 2], NEG_BIG, 0.0).astype(bf16)
            kh_ref[hd, 0:HALO, :] = jnp.broadcast_to(row, (HALO, LANES))
        v_ref[:, 0:HALO, :] = jnp.zeros((N_PAIRS, HALO, LANES), bf16)
        cu_ref[0:CARRY_ROWS, :] = jnp.zeros((CARRY_ROWS, D_CONV), f32)

    @pl.when(jnp.logical_not(is_start))
    def _():
        kh_ref[:, 0:HALO, :] = kh_ref[:, T_BLOCK:T_BLOCK + HALO, :]
        v_ref[:, 0:HALO, :] = v_ref[:, T_BLOCK:T_BLOCK + HALO, :]

    x = x_ref[...]
    r = lax.rsqrt(jnp.mean(x * x, axis=-1, keepdims=True) + EPS)
    h_ref[...] = ((x * r) * ng_ref[...]).astype(bf16)

    def proj(col, width=MXU_TILE):
        return _dot(h_ref[...], win_ref[:, col:col + width])

    scale = ATT_HEAD_DIM ** -0.5
    for c in range(0, D_ATT, MXU_TILE):
        q2 = (proj(_C_Q + c) * scale).astype(bf16)
        k2 = proj(_C_K + c).astype(bf16)
        v2 = proj(_C_V + c).astype(bf16)
        for g in range(MXU_TILE // LANES):
            p = c // LANES + g
            lanes = slice(g * LANES, (g + 1) * LANES)
            qp, kp = q2[:, lanes], k2[:, lanes]
            for j in range(2):
                hd = 2 * p + j
                qx_ref[hd] = jnp.where(own_half[j], qp, jnp.where(offset_lane[j], 1.0, 0.0).astype(bf16))
                kh_ref[hd, HALO:HALO + T_BLOCK, :] = jnp.where(own_half[j], kp, jnp.zeros_like(kp))
            v_ref[p, HALO:HALO + T_BLOCK, :] = v2[:, lanes]

    def attend(b, hd):
        p, j = divmod(hd, 2)
        q0 = b * Q_BLOCK
        s = lax.dot_general(qx_ref[hd, q0:q0 + Q_BLOCK, :], kh_ref[hd, q0:q0 + WIN, :],
                            (((1,), (1,)), ((), ())), preferred_element_type=f32)
        s = s + tbl_ref[hd]
        m = jnp.max(s, axis=-1, keepdims=True)
        e = jnp.exp(s - m)
        l = jnp.sum(e, axis=-1, keepdims=True)
        o = _dot(e.astype(bf16), v_ref[p, q0:q0 + WIN, :]) * (1.0 / l)
        c0 = p * LANES + j * ATT_HEAD_DIM
        att_ref[q0:q0 + Q_BLOCK, c0:c0 + ATT_HEAD_DIM] = o[:, j * ATT_HEAD_DIM:(j + 1) * ATT_HEAD_DIM]

    def z_task(c):
        sz_ref[:, c:c + MXU_TILE] = jax.nn.silu(proj(_C_ZA + c))

    def conv_task(c):
        cols = slice(c, c + MXU_TILE)
        cu = proj(_C_GC + c) * proj(_C_U + c)
        cu_ref[CARRY_ROWS:CARRY_ROWS + T_BLOCK, cols] = cu
        cw = cw_ref[:, cols]
        vconv = (cw[0:1, :] * cu_ref[CARRY_ROWS - 2:CARRY_ROWS - 2 + T_BLOCK, cols]
                 + cw[1:2, :] * cu_ref[CARRY_ROWS - 1:CARRY_ROWS - 1 + T_BLOCK, cols]
                 + cw[2:3, :] * cu) + cb_ref[:, cols]
        cu_ref[0:CARRY_ROWS, cols] = cu_ref[T_BLOCK:T_BLOCK + CARRY_ROWS, cols]
        gconv_ref[:, cols] = (proj(_C_GB + c) * vconv * jax.nn.silu(proj(_C_ZC + c))).astype(bf16)

    def yconv_task(c):
        y_conv = _dot(gconv_ref[...], wco_ref[:, c:c + MXU_TILE])
        ycg_ref[:, c:c + MXU_TILE] = jax.nn.sigmoid(proj(_C_GCV + c)) * y_conv

    def gatt_task(c):
        sga_ref[:, c:c + MXU_TILE] = jax.nn.sigmoid(proj(_C_GA + c))

    dense_tasks = (
        [functools.partial(z_task, c) for c in range(0, D_ATT, MXU_TILE)]
        + [functools.partial(conv_task, c) for c in range(0, D_CONV, MXU_TILE)]
        + [functools.partial(gatt_task, c) for c in range(0, D_MODEL, MXU_TILE)]
        + [functools.partial(yconv_task, c) for c in range(0, D_MODEL, MXU_TILE)]
    )
    units = [(b, hd) for b in range(T_BLOCK // Q_BLOCK) for hd in range(ATT_HEADS)]
    for n, (b, hd) in enumerate(units):
        attend(b, hd)
        if n < len(dense_tasks):
            dense_tasks[n]()
    for task in dense_tasks[len(units):]:
        task()

    gatt = (att_ref[...] * sz_ref[...]).astype(bf16)
    y_att = _dot(gatt, wao_ref[...])
    m = (sga_ref[...] * y_att + ycg_ref[...]).astype(bf16)
    xo = x_ref[...] + _dot(m, wo_ref[...])
    r2 = lax.rsqrt(jnp.mean(xo * xo, axis=-1, keepdims=True) + EPS)
    out_ref[...] = (xo * r2) * fg_ref[...]


def _const_spec(shape):
    zeros = (0,) * len(shape)
    return pl.BlockSpec(shape, lambda i: zeros, pipeline_mode=pl.Buffered(1))


def kernel(x, norm_g, w_in, rel_bias, w_att_out, conv_w, conv_b, w_conv_out, w_out, final_norm_g):
    b, s, d = x.shape
    assert d == D_MODEL and s % T_BLOCK == 0 and norm_g.shape[0] == 1
    bf16 = jnp.bfloat16
    n_tok = b * s
    rb_padded = jnp.pad(rel_bias[0], ((0, 0), (0, REL_PAD - N_REL)))

    operands = (
        x.reshape(n_tok, d), norm_g[0][None, :], w_in[0].astype(bf16), rb_padded,
        w_att_out[0].astype(bf16), conv_w[0], conv_b[0][None, :], w_conv_out[0].astype(bf16),
        w_out[0].astype(bf16), final_norm_g[None, :],
    )
    in_specs = [pl.BlockSpec((T_BLOCK, d), lambda i: (i, 0))] + [_const_spec(o.shape) for o in operands[1:]]
    out = pl.pallas_call(
        functools.partial(_block_kernel, steps_per_seq=s // T_BLOCK),
        grid=(n_tok // T_BLOCK,),
        in_specs=in_specs,
        out_specs=pl.BlockSpec((T_BLOCK, d), lambda i: (i, 0)),
        out_shape=jax.ShapeDtypeStruct((n_tok, d), jnp.float32),
        scratch_shapes=[
            pltpu.VMEM((ATT_HEADS, Q_BLOCK, WIN), jnp.float32),
            pltpu.VMEM((T_BLOCK, D_MODEL), bf16),
            pltpu.VMEM((ATT_HEADS, T_BLOCK, LANES), bf16),
            pltpu.VMEM((ATT_HEADS, HALO + T_BLOCK, LANES), bf16),
            pltpu.VMEM((N_PAIRS, HALO + T_BLOCK, LANES), bf16),
            pltpu.VMEM((T_BLOCK, D_ATT), jnp.float32),
            pltpu.VMEM((T_BLOCK, D_MODEL), jnp.float32),
            pltpu.VMEM((T_BLOCK, D_MODEL), jnp.float32),
            pltpu.VMEM((T_BLOCK, D_ATT), jnp.float32),
            pltpu.VMEM((CARRY_ROWS + T_BLOCK, D_CONV), jnp.float32),
            pltpu.VMEM((T_BLOCK, D_CONV), bf16),
        ],
        compiler_params=pltpu.CompilerParams(
            dimension_semantics=("arbitrary",), vmem_limit_bytes=VMEM_LIMIT_BYTES),
        name="hybrid_block",
    )(*operands)
    return out.reshape(b, s, d)
```

```python
import functools

import jax
import jax.numpy as jnp
from jax import lax
from jax.experimental import pallas as pl
from jax.experimental.pallas import tpu as pltpu

D_MODEL = 1024
CHUNK = 64
N_LEFT_CHUNKS = 8
ATT_HEADS = 8
ATT_HEAD_DIM = 64
D_ATT = ATT_HEADS * ATT_HEAD_DIM
D_CONV = D_MODEL // 2
CONV_WIDTH = 3
MAX_REL = 128
N_REL = 2 * MAX_REL + 1
EPS = 1e-6
NEG_BIG = -1e30
IN_COLS = 4 * D_ATT + 4 * D_CONV + 2 * D_MODEL
LOG2_E = 1.4426950408889634

LANES = 128
HEADS_PER_LANE_GROUP = LANES // ATT_HEAD_DIM
MXU_TILE = 256
DENSE_COLS = 2 * MXU_TILE
T_BLOCK = 512
Q_BLOCK = 256
HALO = N_LEFT_CHUNKS * CHUNK
WIN = HALO + Q_BLOCK
BAND = HALO + CHUNK
SLAB = -(-(BAND + CHUNK) // LANES) * LANES
CARRY_ROWS = 8
CHUNK_SHIFT = CHUNK.bit_length() - 1
REL_PAD = -(-N_REL // LANES) * LANES
VMEM_LIMIT_BYTES = 58 * 1024 * 1024

assert T_BLOCK == HALO and T_BLOCK % Q_BLOCK == 0 and HEADS_PER_LANE_GROUP == 2

_C_Q, _C_K, _C_V, _C_ZA = 0, D_ATT, 2 * D_ATT, 3 * D_ATT
_C_GB = 4 * D_ATT
_C_GC, _C_U, _C_ZC = _C_GB + D_CONV, _C_GB + 2 * D_CONV, _C_GB + 3 * D_CONV
_C_GA = _C_GB + 4 * D_CONV
_C_GCV = _C_GA + D_MODEL


def _dot(a, b):
    return jnp.dot(a, b, preferred_element_type=jnp.float32)


def _build_bias_table(rb_ref, tbl_ref):
    f32, bf16 = jnp.float32, jnp.bfloat16
    rb = rb_ref[...]
    hi = rb.astype(bf16)
    rem = rb - hi.astype(f32)
    mid = rem.astype(bf16)
    lo = (rem - mid.astype(f32)).astype(bf16)
    src = lax.broadcasted_iota(jnp.int32, (REL_PAD, WIN), 0)
    m = lax.broadcasted_iota(jnp.int32, (REL_PAD, WIN), 1)
    idx = jnp.where(m > HALO + MAX_REL, 2 * MAX_REL, jnp.clip(HALO + MAX_REL - m, 0, 2 * MAX_REL))
    onehot = (src == idx).astype(bf16)
    g = _dot(hi, onehot) + _dot(mid, onehot) + _dot(lo, onehot)
    r = lax.broadcasted_iota(jnp.int32, (Q_BLOCK, WIN), 0)
    c = lax.broadcasted_iota(jnp.int32, (Q_BLOCK, WIN), 1)
    dc = (lax.shift_right_logical(r, CHUNK_SHIFT) + N_LEFT_CHUNKS) - lax.shift_right_logical(c, CHUNK_SHIFT)
    valid = jnp.logical_and(dc >= 0, dc <= N_LEFT_CHUNKS)
    for hd in range(ATT_HEADS):
        rows = jnp.broadcast_to(g[hd:hd + 1, :], (Q_BLOCK, WIN))
        toeplitz = pltpu.roll(rows, 0, 1, stride=1, stride_axis=0)
        tbl_ref[hd] = jnp.where(valid, toeplitz * LOG2_E, NEG_BIG)


def _block_kernel(x_ref, xp_ref, ng_ref, win_ref, rb_ref, wao_ref, cw_ref, cb_ref, wco_ref, wo_ref, fg_ref,
                  out_ref, tbl_ref, h_ref, qx_ref, kh_ref, vh_ref, sz_ref, cu_ref, gconv_ref, gatt_ref, m_ref,
                  *, steps_per_seq, n_blocks):
    i = pl.program_id(0)
    has_a = i < n_blocks
    is_start = jnp.logical_and(has_a, (i % steps_per_seq) == 0)
    f32, bf16 = jnp.float32, jnp.bfloat16

    lane = lax.broadcasted_iota(jnp.int32, (1, LANES), 1)
    own_half = [(lane >= j * ATT_HEAD_DIM) & (lane < (j + 1) * ATT_HEAD_DIM) for j in range(2)]
    offset_lane = [lane == (1 - j) * ATT_HEAD_DIM for j in range(2)]

    @pl.when(i == 0)
    def _():
        _build_bias_table(rb_ref, tbl_ref)

    @pl.when(is_start)
    def _():
        for hd in range(ATT_HEADS):
            row = jnp.where(offset_lane[hd % 2], NEG_BIG, 0.0).astype(bf16)
            kh_ref[hd, 0:HALO, :] = jnp.broadcast_to(row, (HALO, LANES))
        vh_ref[:, 0:HALO, :] = jnp.zeros((ATT_HEADS, HALO, LANES), bf16)
        cu_ref[0:CARRY_ROWS, :] = jnp.zeros((CARRY_ROWS, D_CONV), f32)

    @pl.when(jnp.logical_and(has_a, jnp.logical_not(is_start)))
    def _():
        kh_ref[:, 0:HALO, :] = kh_ref[:, T_BLOCK:T_BLOCK + HALO, :]
        vh_ref[:, 0:HALO, :] = vh_ref[:, T_BLOCK:T_BLOCK + HALO, :]

    def proj(col):
        return _dot(h_ref[...], win_ref[:, col:col + DENSE_COLS])

    def a_norm():
        x = x_ref[...]
        r = lax.rsqrt(jnp.mean(x * x, axis=-1, keepdims=True) + EPS)
        h_ref[...] = ((x * r) * ng_ref[...]).astype(bf16)

    def a_qkv(c):
        q2 = (proj(_C_Q + c) * (ATT_HEAD_DIM ** -0.5 * LOG2_E)).astype(bf16)
        k2 = proj(_C_K + c).astype(bf16)
        v2 = proj(_C_V + c).astype(bf16)
        for g in range(DENSE_COLS // LANES):
            lanes = slice(g * LANES, (g + 1) * LANES)
            qp, kp, vp = q2[:, lanes], k2[:, lanes], v2[:, lanes]
            for j in range(2):
                hd = 2 * (c // LANES + g) + j
                qx_ref[hd] = jnp.where(own_half[j], qp, jnp.where(offset_lane[j], 1.0, 0.0).astype(bf16))
                kh_ref[hd, HALO:HALO + T_BLOCK, :] = jnp.where(own_half[j], kp, jnp.zeros_like(kp))
                vh_ref[hd, HALO:HALO + T_BLOCK, :] = jnp.where(own_half[j], vp, jnp.ones_like(vp))

    def a_z(c):
        sz_ref[:, c:c + DENSE_COLS] = jax.nn.silu(proj(_C_ZA + c))

    def a_conv(c):
        cols = slice(c, c + DENSE_COLS)
        cu = proj(_C_GC + c) * proj(_C_U + c)
        cu_ref[CARRY_ROWS:CARRY_ROWS + T_BLOCK, cols] = cu
        cw = cw_ref[:, cols]
        vconv = (cw[0:1, :] * cu_ref[CARRY_ROWS - 2:CARRY_ROWS - 2 + T_BLOCK, cols]
                 + cw[1:2, :] * cu_ref[CARRY_ROWS - 1:CARRY_ROWS - 1 + T_BLOCK, cols]
                 + cw[2:3, :] * cu) + cb_ref[:, cols]
        cu_ref[0:CARRY_ROWS, cols] = cu_ref[T_BLOCK:T_BLOCK + CARRY_ROWS, cols]
        gconv_ref[:, cols] = (proj(_C_GB + c) * vconv * jax.nn.silu(proj(_C_ZC + c))).astype(bf16)

    def scores(b, hd):
        q0 = b * Q_BLOCK
        return lax.dot_general(qx_ref[hd, q0:q0 + Q_BLOCK, :], kh_ref[hd, q0:q0 + WIN, :],
                               (((1,), (1,)), ((), ())), preferred_element_type=f32)

    def probabilities(s, hd):
        slabs = []
        for a in range(Q_BLOCK // CHUNK):
            rows = slice(a * CHUNK, (a + 1) * CHUNK)
            lo = a * CHUNK // LANES * LANES
            sa = s[rows, lo:lo + SLAB] + tbl_ref[hd, rows, lo:lo + SLAB]
            ea = jnp.exp2(sa - jnp.max(sa, axis=-1, keepdims=True)).astype(bf16)
            pieces = ([jnp.zeros((CHUNK, lo), bf16)] if lo else []) + [ea]
            if lo + SLAB < WIN:
                pieces.append(jnp.zeros((CHUNK, WIN - lo - SLAB), bf16))
            slabs.append(jnp.concatenate(pieces, axis=1))
        return jnp.concatenate(slabs, axis=0)

    def weighted_values(e, b, hd):
        q0 = b * Q_BLOCK
        o = _dot(e, vh_ref[hd, q0:q0 + WIN, :])
        return o * (1.0 / pltpu.roll(o, ATT_HEAD_DIM, 1))

    def a_pair(b, p):
        rows = slice(b * Q_BLOCK, (b + 1) * Q_BLOCK)
        lanes = slice(p * LANES, (p + 1) * LANES)
        heads = (2 * p, 2 * p + 1)
        s = [scores(b, hd) for hd in heads]
        e = [probabilities(s[j], hd) for j, hd in enumerate(heads)]
        o = [weighted_values(e[j], b, hd) for j, hd in enumerate(heads)]
        gatt_ref[rows, lanes] = (jnp.where(own_half[0], o[0], o[1]) * sz_ref[rows, lanes]).astype(bf16)

    def b_merge(c):
        cols = slice(c, c + DENSE_COLS)
        y_att = _dot(gatt_ref[...], wao_ref[:, cols])
        y_conv = _dot(gconv_ref[...], wco_ref[:, cols])
        m = jax.nn.sigmoid(proj(_C_GA + c)) * y_att + jax.nn.sigmoid(proj(_C_GCV + c)) * y_conv
        m_ref[:, cols] = m.astype(bf16)

    def b_out(c):
        cols = slice(c, c + DENSE_COLS)
        out_ref[:, cols] = xp_ref[:, cols] + _dot(m_ref[...], wo_ref[:, cols])

    def b_final():
        xo = out_ref[...]
        r2 = lax.rsqrt(jnp.mean(xo * xo, axis=-1, keepdims=True) + EPS)
        out_ref[...] = (xo * r2) * fg_ref[...]

    P = functools.partial
    col_tiles = range(0, D_MODEL, DENSE_COLS)
    n_pairs = ATT_HEADS // 2
    b_merges = [P(b_merge, c) for c in col_tiles]
    b_outs = [P(b_out, c) for c in col_tiles]
    a_head = ([a_norm] + [P(a_qkv, c) for c in range(0, D_ATT, DENSE_COLS)]
              + [P(a_z, c) for c in range(0, D_ATT, DENSE_COLS)])
    a_pairs = [P(a_pair, b, p) for b in range(T_BLOCK // Q_BLOCK) for p in range(n_pairs)]
    a_convs = [P(a_conv, c) for c in range(0, D_CONV, DENSE_COLS)]

    def run(tasks):
        for task in tasks:
            task()

    @pl.when(i == 0)
    def _():
        run(a_head + a_pairs[:n_pairs] + a_convs + a_pairs[n_pairs:])

    @pl.when(jnp.logical_and(i > 0, has_a))
    def _():
        run(b_merges + a_head
            + [a_pairs[0], b_outs[0], a_pairs[1], b_outs[1], a_pairs[2], a_pairs[3], b_final]
            + a_pairs[n_pairs:] + a_convs)

    @pl.when(i == n_blocks)
    def _():
        run(b_merges + b_outs + [b_final])


def _const_spec(shape):
    zeros = (0,) * len(shape)
    return pl.BlockSpec(shape, lambda i: zeros, pipeline_mode=pl.Buffered(1))


def kernel(x, norm_g, w_in, rel_bias, w_att_out, conv_w, conv_b, w_conv_out, w_out, final_norm_g):
    b, s, d = x.shape
    assert d == D_MODEL and s % T_BLOCK == 0 and norm_g.shape[0] == 1
    bf16 = jnp.bfloat16
    n_tok = b * s
    n_blocks = n_tok // T_BLOCK
    rb_padded = jnp.pad(rel_bias[0], ((0, 0), (0, REL_PAD - N_REL)))
    xf = x.reshape(n_tok, d)

    consts = (
        norm_g[0][None, :], w_in[0].astype(bf16), rb_padded,
        w_att_out[0].astype(bf16), conv_w[0], conv_b[0][None, :], w_conv_out[0].astype(bf16),
        w_out[0].astype(bf16), final_norm_g[None, :],
    )
    cur_spec = pl.BlockSpec((T_BLOCK, d), lambda i: (jnp.minimum(i, n_blocks - 1), 0))
    prev_spec = pl.BlockSpec((T_BLOCK, d), lambda i: (jnp.maximum(i - 1, 0), 0))
    out = pl.pallas_call(
        functools.partial(_block_kernel, steps_per_seq=s // T_BLOCK, n_blocks=n_blocks),
        grid=(n_blocks + 1,),
        in_specs=[cur_spec, prev_spec] + [_const_spec(o.shape) for o in consts],
        out_specs=prev_spec,
        out_shape=jax.ShapeDtypeStruct((n_tok, d), jnp.float32),
        scratch_shapes=[
            pltpu.VMEM((ATT_HEADS, Q_BLOCK, WIN), jnp.float32),
            pltpu.VMEM((T_BLOCK, D_MODEL), bf16),
            pltpu.VMEM((ATT_HEADS, T_BLOCK, LANES), bf16),
            pltpu.VMEM((ATT_HEADS, HALO + T_BLOCK, LANES), bf16),
            pltpu.VMEM((ATT_HEADS, HALO + T_BLOCK, LANES), bf16),
            pltpu.VMEM((T_BLOCK, D_ATT), jnp.float32),
            pltpu.VMEM((CARRY_ROWS + T_BLOCK, D_CONV), jnp.float32),
            pltpu.VMEM((T_BLOCK, D_CONV), bf16),
            pltpu.VMEM((T_BLOCK, D_ATT), bf16),
            pltpu.VMEM((T_BLOCK, D_MODEL), bf16),
        ],
        compiler_params=pltpu.CompilerParams(
            dimension_semantics=("arbitrary",), vmem_limit_bytes=VMEM_LIMIT_BYTES),
        name="hybrid_block",
    )(xf, xf, *consts)
    return out.reshape(b, s, d)
```

```python
import functools

import jax
import jax.numpy as jnp
from jax import lax
from jax.experimental import pallas as pl
from jax.experimental.pallas import tpu as pltpu

D_MODEL = 1024
CHUNK = 64
N_LEFT_CHUNKS = 8
ATT_HEADS = 8
ATT_HEAD_DIM = 64
D_ATT = ATT_HEADS * ATT_HEAD_DIM
D_CONV = D_MODEL // 2
CONV_WIDTH = 3
MAX_REL = 128
N_REL = 2 * MAX_REL + 1
EPS = 1e-6
NEG_BIG = -1e30
IN_COLS = 4 * D_ATT + 4 * D_CONV + 2 * D_MODEL
LOG2_E = 1.4426950408889634

LANES = 128
HEADS_PER_LANE_GROUP = LANES // ATT_HEAD_DIM
MXU_TILE = 256
DENSE_COLS = 2 * MXU_TILE
T_BLOCK = 512
Q_BLOCK = 256
HALO = N_LEFT_CHUNKS * CHUNK
WIN = HALO + Q_BLOCK
BAND = HALO + CHUNK
SLAB = -(-(BAND + CHUNK) // LANES) * LANES
CARRY_ROWS = 8
CHUNK_SHIFT = CHUNK.bit_length() - 1
REL_PAD = -(-N_REL // LANES) * LANES
VMEM_LIMIT_BYTES = 58 * 1024 * 1024

assert T_BLOCK == HALO and T_BLOCK % Q_BLOCK == 0 and HEADS_PER_LANE_GROUP == 2

_C_Q, _C_K, _C_V, _C_ZA = 0, D_ATT, 2 * D_ATT, 3 * D_ATT
_C_GB = 4 * D_ATT
_C_GC, _C_U, _C_ZC = _C_GB + D_CONV, _C_GB + 2 * D_CONV, _C_GB + 3 * D_CONV
_C_GA = _C_GB + 4 * D_CONV
_C_GCV = _C_GA + D_MODEL


def _dot(a, b):
    return jnp.dot(a, b, preferred_element_type=jnp.float32)


def _build_bias_table(rb_ref, tbl_ref):
    f32, bf16 = jnp.float32, jnp.bfloat16
    rb = rb_ref[...]
    hi = rb.astype(bf16)
    rem = rb - hi.astype(f32)
    mid = rem.astype(bf16)
    lo = (rem - mid.astype(f32)).astype(bf16)
    src = lax.broadcasted_iota(jnp.int32, (REL_PAD, WIN), 0)
    m = lax.broadcasted_iota(jnp.int32, (REL_PAD, WIN), 1)
    idx = jnp.where(m > HALO + MAX_REL, 2 * MAX_REL, jnp.clip(HALO + MAX_REL - m, 0, 2 * MAX_REL))
    onehot = (src == idx).astype(bf16)
    g = _dot(hi, onehot) + _dot(mid, onehot) + _dot(lo, onehot)
    r = lax.broadcasted_iota(jnp.int32, (Q_BLOCK, WIN), 0)
    c = lax.broadcasted_iota(jnp.int32, (Q_BLOCK, WIN), 1)
    dc = (lax.shift_right_logical(r, CHUNK_SHIFT) + N_LEFT_CHUNKS) - lax.shift_right_logical(c, CHUNK_SHIFT)
    valid = jnp.logical_and(dc >= 0, dc <= N_LEFT_CHUNKS)
    for hd in range(ATT_HEADS):
        rows = jnp.broadcast_to(g[hd:hd + 1, :], (Q_BLOCK, WIN))
        toeplitz = pltpu.roll(rows, 0, 1, stride=1, stride_axis=0)
        tbl_ref[hd] = jnp.where(valid, toeplitz * LOG2_E, NEG_BIG).T


def _block_kernel(x_ref, xp_ref, ng_ref, win_ref, wvt_ref, rb_ref, wao_ref, cw_ref, cb_ref, wco_ref, wo_ref, fg_ref,
                  out_ref, tbl_ref, h_ref, qx_ref, kh_ref, vht_ref, sz_ref, cu_ref, gconv_ref, gatt_ref, m_ref,
                  *, steps_per_seq, n_blocks):
    i = pl.program_id(0)
    has_a = i < n_blocks
    is_start = jnp.logical_and(has_a, (i % steps_per_seq) == 0)
    f32, bf16 = jnp.float32, jnp.bfloat16

    lane = lax.broadcasted_iota(jnp.int32, (1, LANES), 1)
    own_half = [(lane >= j * ATT_HEAD_DIM) & (lane < (j + 1) * ATT_HEAD_DIM) for j in range(2)]
    offset_lane = [lane == (1 - j) * ATT_HEAD_DIM for j in range(2)]
    sub = lax.broadcasted_iota(jnp.int32, (LANES, 1), 0)
    own_rows = [(sub >= j * ATT_HEAD_DIM) & (sub < (j + 1) * ATT_HEAD_DIM) for j in range(2)]

    @pl.when(i == 0)
    def _():
        _build_bias_table(rb_ref, tbl_ref)

    @pl.when(is_start)
    def _():
        for hd in range(ATT_HEADS):
            row = jnp.where(offset_lane[hd % 2], NEG_BIG, 0.0).astype(bf16)
            kh_ref[hd, 0:HALO, :] = jnp.broadcast_to(row, (HALO, LANES))
        vht_ref[:, :, 0:HALO] = jnp.zeros((ATT_HEADS, LANES, HALO), bf16)
        cu_ref[0:CARRY_ROWS, :] = jnp.zeros((CARRY_ROWS, D_CONV), f32)

    @pl.when(jnp.logical_and(has_a, jnp.logical_not(is_start)))
    def _():
        kh_ref[:, 0:HALO, :] = kh_ref[:, T_BLOCK:T_BLOCK + HALO, :]
        vht_ref[:, :, 0:HALO] = vht_ref[:, :, T_BLOCK:T_BLOCK + HALO]

    def proj(col):
        return _dot(h_ref[...], win_ref[:, col:col + DENSE_COLS])

    def a_norm():
        x = x_ref[...]
        r = lax.rsqrt(jnp.mean(x * x, axis=-1, keepdims=True) + EPS)
        h_ref[...] = ((x * r) * ng_ref[...]).astype(bf16)

    def a_qkv(c):
        q2 = (proj(_C_Q + c) * (ATT_HEAD_DIM ** -0.5 * LOG2_E)).astype(bf16)
        k2 = proj(_C_K + c).astype(bf16)
        vt = lax.dot_general(wvt_ref[c:c + DENSE_COLS, :], h_ref[...], (((1,), (1,)), ((), ())),
                             preferred_element_type=f32).astype(bf16)
        for g in range(DENSE_COLS // LANES):
            lanes = slice(g * LANES, (g + 1) * LANES)
            qp, kp, vtp = q2[:, lanes], k2[:, lanes], vt[lanes, :]
            for j in range(2):
                hd = 2 * (c // LANES + g) + j
                qx_ref[hd] = jnp.where(own_half[j], qp, jnp.where(offset_lane[j], 1.0, 0.0).astype(bf16))
                kh_ref[hd, HALO:HALO + T_BLOCK, :] = jnp.where(own_half[j], kp, jnp.zeros_like(kp))
                vht_ref[hd, :, HALO:HALO + T_BLOCK] = jnp.where(own_rows[j], vtp, jnp.ones_like(vtp))

    def a_z(c):
        sz_ref[:, c:c + DENSE_COLS] = jax.nn.silu(proj(_C_ZA + c))

    def a_conv(c):
        cols = slice(c, c + DENSE_COLS)
        cu = proj(_C_GC + c) * proj(_C_U + c)
        cu_ref[CARRY_ROWS:CARRY_ROWS + T_BLOCK, cols] = cu
        cw = cw_ref[:, cols]
        vconv = (cw[0:1, :] * cu_ref[CARRY_ROWS - 2:CARRY_ROWS - 2 + T_BLOCK, cols]
                 + cw[1:2, :] * cu_ref[CARRY_ROWS - 1:CARRY_ROWS - 1 + T_BLOCK, cols]
                 + cw[2:3, :] * cu) + cb_ref[:, cols]
        cu_ref[0:CARRY_ROWS, cols] = cu_ref[T_BLOCK:T_BLOCK + CARRY_ROWS, cols]
        gconv_ref[:, cols] = (proj(_C_GB + c) * vconv * jax.nn.silu(proj(_C_ZC + c))).astype(bf16)

    def scores_t(b, hd):
        q0 = b * Q_BLOCK
        return lax.dot_general(kh_ref[hd, q0:q0 + WIN, :], qx_ref[hd, q0:q0 + Q_BLOCK, :],
                               (((1,), (1,)), ((), ())), preferred_element_type=f32)

    def probabilities_t(st, hd):
        cols = []
        for g in range(Q_BLOCK // LANES):
            lanes = slice(g * LANES, (g + 1) * LANES)
            lo = g * LANES
            sg = st[lo:lo + SLAB, lanes] + tbl_ref[hd, lo:lo + SLAB, lanes]
            eg = jnp.exp2(sg - jnp.max(sg, axis=0, keepdims=True)).astype(bf16)
            pieces = ([jnp.zeros((lo, LANES), bf16)] if lo else []) + [eg]
            if lo + SLAB < WIN:
                pieces.append(jnp.zeros((WIN - lo - SLAB, LANES), bf16))
            cols.append(jnp.concatenate(pieces, axis=0))
        return jnp.concatenate(cols, axis=1)

    def weighted_values_t(et, b, hd):
        q0 = b * Q_BLOCK
        j = hd % 2
        ot = _dot(vht_ref[hd, :, q0:q0 + WIN], et)
        own = ot[j * ATT_HEAD_DIM:(j + 1) * ATT_HEAD_DIM, :]
        sums = ot[(1 - j) * ATT_HEAD_DIM:(2 - j) * ATT_HEAD_DIM, :]
        return own * (1.0 / sums)

    def attention_slots():
        units = [(b, p) for b in range(T_BLOCK // Q_BLOCK) for p in range(ATT_HEADS // 2)]
        state = {}

        def s_phase(n):
            b, p = units[n]
            state[n] = [scores_t(b, hd) for hd in (2 * p, 2 * p + 1)]

        def e_phase(n):
            p = units[n][1]
            state[n] = [probabilities_t(st, hd) for st, hd in zip(state[n], (2 * p, 2 * p + 1))]

        def v_phase(n):
            b, p = units[n]
            rows = slice(b * Q_BLOCK, (b + 1) * Q_BLOCK)
            lanes = slice(p * LANES, (p + 1) * LANES)
            ot = [weighted_values_t(et, b, hd) for et, hd in zip(state.pop(n), (2 * p, 2 * p + 1))]
            att = jnp.concatenate(ot, axis=0).T
            gatt_ref[rows, lanes] = (att * sz_ref[rows, lanes]).astype(bf16)

        phases = (s_phase, e_phase, v_phase)
        return [[functools.partial(ph, t - d) for d, ph in enumerate(phases) if 0 <= t - d < len(units)]
                for t in range(len(units) + len(phases) - 1)]

    def b_merge(c):
        cols = slice(c, c + DENSE_COLS)
        y_att = _dot(gatt_ref[...], wao_ref[:, cols])
        y_conv = _dot(gconv_ref[...], wco_ref[:, cols])
        m = jax.nn.sigmoid(proj(_C_GA + c)) * y_att + jax.nn.sigmoid(proj(_C_GCV + c)) * y_conv
        m_ref[:, cols] = m.astype(bf16)

    def b_out(c):
        cols = slice(c, c + DENSE_COLS)
        out_ref[:, cols] = xp_ref[:, cols] + _dot(m_ref[...], wo_ref[:, cols])

    def b_final():
        xo = out_ref[...]
        r2 = lax.rsqrt(jnp.mean(xo * xo, axis=-1, keepdims=True) + EPS)
        out_ref[...] = (xo * r2) * fg_ref[...]

    P = functools.partial
    col_tiles = range(0, D_MODEL, DENSE_COLS)
    b_merges = [P(b_merge, c) for c in col_tiles]
    b_outs = [P(b_out, c) for c in col_tiles]
    a_head = ([a_norm] + [P(a_qkv, c) for c in range(0, D_ATT, DENSE_COLS)]
              + [P(a_z, c) for c in range(0, D_ATT, DENSE_COLS)])
    a_convs = [P(a_conv, c) for c in range(0, D_CONV, DENSE_COLS)]

    def run(tasks):
        for task in tasks:
            task()

    def interleave(slots, extras):
        order = []
        for t, slot in enumerate(slots):
            order += slot + extras.get(t, [])
        return order

    @pl.when(i == 0)
    def _():
        run(a_head + interleave(attention_slots(), {4: a_convs}))

    @pl.when(jnp.logical_and(i > 0, has_a))
    def _():
        extras = {1: b_outs[:1], 2: b_outs[1:], 4: [b_final], 7: a_convs}
        run(b_merges + a_head + interleave(attention_slots(), extras))

    @pl.when(i == n_blocks)
    def _():
        run(b_merges + b_outs + [b_final])


def _const_spec(shape):
    zeros = (0,) * len(shape)
    return pl.BlockSpec(shape, lambda i: zeros, pipeline_mode=pl.Buffered(1))


def kernel(x, norm_g, w_in, rel_bias, w_att_out, conv_w, conv_b, w_conv_out, w_out, final_norm_g):
    b, s, d = x.shape
    assert d == D_MODEL and s % T_BLOCK == 0 and norm_g.shape[0] == 1
    bf16 = jnp.bfloat16
    n_tok = b * s
    n_blocks = n_tok // T_BLOCK
    rb_padded = jnp.pad(rel_bias[0], ((0, 0), (0, REL_PAD - N_REL)))
    xf = x.reshape(n_tok, d)

    consts = (
        norm_g[0][None, :], w_in[0].astype(bf16), w_in[0][:, _C_V:_C_ZA].T.astype(bf16), rb_padded,
        w_att_out[0].astype(bf16), conv_w[0], conv_b[0][None, :], w_conv_out[0].astype(bf16),
        w_out[0].astype(bf16), final_norm_g[None, :],
    )
    cur_spec = pl.BlockSpec((T_BLOCK, d), lambda i: (jnp.minimum(i, n_blocks - 1), 0))
    prev_spec = pl.BlockSpec((T_BLOCK, d), lambda i: (jnp.maximum(i - 1, 0), 0))
    out = pl.pallas_call(
        functools.partial(_block_kernel, steps_per_seq=s // T_BLOCK, n_blocks=n_blocks),
        grid=(n_blocks + 1,),
        in_specs=[cur_spec, prev_spec] + [_const_spec(o.shape) for o in consts],
        out_specs=prev_spec,
        out_shape=jax.ShapeDtypeStruct((n_tok, d), jnp.float32),
        scratch_shapes=[
            pltpu.VMEM((ATT_HEADS, WIN, Q_BLOCK), jnp.float32),
            pltpu.VMEM((T_BLOCK, D_MODEL), bf16),
            pltpu.VMEM((ATT_HEADS, T_BLOCK, LANES), bf16),
            pltpu.VMEM((ATT_HEADS, HALO + T_BLOCK, LANES), bf16),
            pltpu.VMEM((ATT_HEADS, LANES, HALO + T_BLOCK), bf16),
            pltpu.VMEM((T_BLOCK, D_ATT), jnp.float32),
            pltpu.VMEM((CARRY_ROWS + T_BLOCK, D_CONV), jnp.float32),
            pltpu.VMEM((T_BLOCK, D_CONV), bf16),
            pltpu.VMEM((T_BLOCK, D_ATT), bf16),
            pltpu.VMEM((T_BLOCK, D_MODEL), bf16),
        ],
        compiler_params=pltpu.CompilerParams(
            dimension_semantics=("arbitrary",), vmem_limit_bytes=VMEM_LIMIT_BYTES),
        name="hybrid_block",
    )(xf, xf, *consts)
    return out.reshape(b, s, d)
```

```python
import functools

import jax
import jax.numpy as jnp
from jax import lax
from jax.experimental import pallas as pl
from jax.experimental.pallas import tpu as pltpu

D_MODEL = 1024
CHUNK = 64
N_LEFT_CHUNKS = 8
ATT_HEADS = 8
ATT_HEAD_DIM = 64
D_ATT = ATT_HEADS * ATT_HEAD_DIM
D_CONV = D_MODEL // 2
CONV_WIDTH = 3
MAX_REL = 128
N_REL = 2 * MAX_REL + 1
EPS = 1e-6
NEG_BIG = -1e30
IN_COLS = 4 * D_ATT + 4 * D_CONV + 2 * D_MODEL
LOG2_E = 1.4426950408889634

LANES = 128
HEADS_PER_LANE_GROUP = LANES // ATT_HEAD_DIM
MXU_TILE = 256
DENSE_COLS = 2 * MXU_TILE
T_BLOCK = 512
Q_BLOCK = 256
HALO = N_LEFT_CHUNKS * CHUNK
WIN = HALO + Q_BLOCK
BAND = HALO + CHUNK
SLAB = -(-(BAND + CHUNK) // LANES) * LANES
CARRY_ROWS = 8
CHUNK_SHIFT = CHUNK.bit_length() - 1
REL_PAD = -(-N_REL // LANES) * LANES
VMEM_LIMIT_BYTES = 58 * 1024 * 1024

assert T_BLOCK == HALO and T_BLOCK % Q_BLOCK == 0 and HEADS_PER_LANE_GROUP == 2

_C_Q, _C_K, _C_V, _C_ZA = 0, D_ATT, 2 * D_ATT, 3 * D_ATT
_C_GB = 4 * D_ATT
_C_GC, _C_U, _C_ZC = _C_GB + D_CONV, _C_GB + 2 * D_CONV, _C_GB + 3 * D_CONV
_C_GA = _C_GB + 4 * D_CONV
_C_GCV = _C_GA + D_MODEL


def _dot(a, b):
    return jnp.dot(a, b, preferred_element_type=jnp.float32)


def _build_bias_table(rb_ref, tbl_ref):
    f32, bf16 = jnp.float32, jnp.bfloat16
    rb = rb_ref[...]
    hi = rb.astype(bf16)
    rem = rb - hi.astype(f32)
    mid = rem.astype(bf16)
    lo = (rem - mid.astype(f32)).astype(bf16)
    src = lax.broadcasted_iota(jnp.int32, (REL_PAD, WIN), 0)
    m = lax.broadcasted_iota(jnp.int32, (REL_PAD, WIN), 1)
    idx = jnp.where(m > HALO + MAX_REL, 2 * MAX_REL, jnp.clip(HALO + MAX_REL - m, 0, 2 * MAX_REL))
    onehot = (src == idx).astype(bf16)
    g = _dot(hi, onehot) + _dot(mid, onehot) + _dot(lo, onehot)
    r = lax.broadcasted_iota(jnp.int32, (Q_BLOCK, WIN), 0)
    c = lax.broadcasted_iota(jnp.int32, (Q_BLOCK, WIN), 1)
    dc = (lax.shift_right_logical(r, CHUNK_SHIFT) + N_LEFT_CHUNKS) - lax.shift_right_logical(c, CHUNK_SHIFT)
    valid = jnp.logical_and(dc >= 0, dc <= N_LEFT_CHUNKS)
    for hd in range(ATT_HEADS):
        rows = jnp.broadcast_to(g[hd:hd + 1, :], (Q_BLOCK, WIN))
        toeplitz = pltpu.roll(rows, 0, 1, stride=1, stride_axis=0)
        tbl_ref[hd] = jnp.where(valid, toeplitz * LOG2_E, NEG_BIG).T


def _block_kernel(x_ref, xp_ref, ng_ref, win_ref, wvt_ref, rb_ref, wao_ref, cw_ref, cb_ref, wco_ref, wo_ref, fg_ref,
                  out_ref, tbl_ref, h_ref, qx_ref, kh_ref, vht_ref, sz_ref, cu_ref, gconv_ref, gatt_ref, m_ref,
                  *, steps_per_seq, n_blocks):
    i = pl.program_id(0)
    has_a = i < n_blocks
    is_start = jnp.logical_and(has_a, (i % steps_per_seq) == 0)
    f32, bf16 = jnp.float32, jnp.bfloat16

    lane = lax.broadcasted_iota(jnp.int32, (1, LANES), 1)
    own_half = [(lane >= j * ATT_HEAD_DIM) & (lane < (j + 1) * ATT_HEAD_DIM) for j in range(2)]
    offset_lane = [lane == (1 - j) * ATT_HEAD_DIM for j in range(2)]
    sub = lax.broadcasted_iota(jnp.int32, (LANES, 1), 0)
    own_rows = [(sub >= j * ATT_HEAD_DIM) & (sub < (j + 1) * ATT_HEAD_DIM) for j in range(2)]

    @pl.when(i == 0)
    def _():
        _build_bias_table(rb_ref, tbl_ref)

    @pl.when(is_start)
    def _():
        for hd in range(ATT_HEADS):
            row = jnp.where(offset_lane[hd % 2], NEG_BIG, 0.0).astype(bf16)
            kh_ref[hd, 0:HALO, :] = jnp.broadcast_to(row, (HALO, LANES))
        vht_ref[:, :, 0:HALO] = jnp.zeros((ATT_HEADS, LANES, HALO), bf16)
        cu_ref[0:CARRY_ROWS, :] = jnp.zeros((CARRY_ROWS, D_CONV), f32)

    @pl.when(jnp.logical_and(has_a, jnp.logical_not(is_start)))
    def _():
        kh_ref[:, 0:HALO, :] = kh_ref[:, T_BLOCK:T_BLOCK + HALO, :]
        vht_ref[:, :, 0:HALO] = vht_ref[:, :, T_BLOCK:T_BLOCK + HALO]

    def proj(col):
        return _dot(h_ref[...], win_ref[:, col:col + DENSE_COLS])

    def a_norm():
        x = x_ref[...]
        r = lax.rsqrt(jnp.mean(x * x, axis=-1, keepdims=True) + EPS)
        h_ref[...] = ((x * r) * ng_ref[...]).astype(bf16)

    def a_qkv(c):
        q2 = (proj(_C_Q + c) * (ATT_HEAD_DIM ** -0.5 * LOG2_E)).astype(bf16)
        k2 = proj(_C_K + c).astype(bf16)
        vt = lax.dot_general(wvt_ref[c:c + DENSE_COLS, :], h_ref[...], (((1,), (1,)), ((), ())),
                             preferred_element_type=f32).astype(bf16)
        for g in range(DENSE_COLS // LANES):
            lanes = slice(g * LANES, (g + 1) * LANES)
            qp, kp, vtp = q2[:, lanes], k2[:, lanes], vt[lanes, :]
            for j in range(2):
                hd = 2 * (c // LANES + g) + j
                qx_ref[hd] = jnp.where(own_half[j], qp, jnp.where(offset_lane[j], 1.0, 0.0).astype(bf16))
                kh_ref[hd, HALO:HALO + T_BLOCK, :] = jnp.where(own_half[j], kp, jnp.zeros_like(kp))
                vht_ref[hd, :, HALO:HALO + T_BLOCK] = jnp.where(own_rows[j], vtp, jnp.ones_like(vtp))

    def a_z(c):
        sz_ref[:, c:c + DENSE_COLS] = jax.nn.silu(proj(_C_ZA + c))

    def a_conv(c):
        cols = slice(c, c + DENSE_COLS)
        cu = proj(_C_GC + c) * proj(_C_U + c)
        cu_ref[CARRY_ROWS:CARRY_ROWS + T_BLOCK, cols] = cu
        cw = cw_ref[:, cols]
        vconv = (cw[0:1, :] * cu_ref[CARRY_ROWS - 2:CARRY_ROWS - 2 + T_BLOCK, cols]
                 + cw[1:2, :] * cu_ref[CARRY_ROWS - 1:CARRY_ROWS - 1 + T_BLOCK, cols]
                 + cw[2:3, :] * cu) + cb_ref[:, cols]
        cu_ref[0:CARRY_ROWS, cols] = cu_ref[T_BLOCK:T_BLOCK + CARRY_ROWS, cols]
        gconv_ref[:, cols] = (proj(_C_GB + c) * vconv * jax.nn.silu(proj(_C_ZC + c))).astype(bf16)

    def scores_t(b, hd):
        q0 = b * Q_BLOCK
        return lax.dot_general(kh_ref[hd, q0:q0 + WIN, :], qx_ref[hd, q0:q0 + Q_BLOCK, :],
                               (((1,), (1,)), ((), ())), preferred_element_type=f32)

    def probabilities_t(st, hd):
        cols = []
        for g in range(Q_BLOCK // LANES):
            lanes = slice(g * LANES, (g + 1) * LANES)
            lo = g * LANES
            sg = st[lo:lo + SLAB, lanes] + tbl_ref[hd, lo:lo + SLAB, lanes]
            eg = jnp.exp2(sg - jnp.max(sg, axis=0, keepdims=True)).astype(bf16)
            pieces = ([jnp.zeros((lo, LANES), bf16)] if lo else []) + [eg]
            if lo + SLAB < WIN:
                pieces.append(jnp.zeros((WIN - lo - SLAB, LANES), bf16))
            cols.append(jnp.concatenate(pieces, axis=0))
        return jnp.concatenate(cols, axis=1)

    def weighted_values_t(et, b, hd):
        q0 = b * Q_BLOCK
        j = hd % 2
        ot = _dot(vht_ref[hd, :, q0:q0 + WIN], et)
        own = ot[j * ATT_HEAD_DIM:(j + 1) * ATT_HEAD_DIM, :]
        sums = ot[(1 - j) * ATT_HEAD_DIM:(2 - j) * ATT_HEAD_DIM, :]
        return own * (1.0 / sums)

    def attention_slots():
        units = [(b, p) for b in range(T_BLOCK // Q_BLOCK) for p in range(ATT_HEADS // 2)]
        state = {}

        def s_phase(n):
            b, p = units[n]
            state[n] = [scores_t(b, hd) for hd in (2 * p, 2 * p + 1)]

        def e_phase(n):
            p = units[n][1]
            state[n] = [probabilities_t(st, hd) for st, hd in zip(state[n], (2 * p, 2 * p + 1))]

        def v_phase(n):
            b, p = units[n]
            rows = slice(b * Q_BLOCK, (b + 1) * Q_BLOCK)
            lanes = slice(p * LANES, (p + 1) * LANES)
            ot = [weighted_values_t(et, b, hd) for et, hd in zip(state.pop(n), (2 * p, 2 * p + 1))]
            att = jnp.concatenate(ot, axis=0).T
            gatt_ref[rows, lanes] = (att * sz_ref[rows, lanes]).astype(bf16)

        phases = (s_phase, e_phase, v_phase)
        return [[functools.partial(ph, t - d) for d, ph in enumerate(phases) if 0 <= t - d < len(units)]
                for t in range(len(units) + len(phases) - 1)]

    def b_merge(c):
        cols = slice(c, c + DENSE_COLS)
        y_att = _dot(gatt_ref[...], wao_ref[:, cols])
        y_conv = _dot(gconv_ref[...], wco_ref[:, cols])
        m = jax.nn.sigmoid(proj(_C_GA + c)) * y_att + jax.nn.sigmoid(proj(_C_GCV + c)) * y_conv
        m_ref[:, cols] = m.astype(bf16)

    def b_out(c):
        cols = slice(c, c + DENSE_COLS)
        out_ref[:, cols] = xp_ref[:, cols] + _dot(m_ref[...], wo_ref[:, cols])

    def b_final():
        xo = out_ref[...]
        r2 = lax.rsqrt(jnp.mean(xo * xo, axis=-1, keepdims=True) + EPS)
        out_ref[...] = (xo * r2) * fg_ref[...]

    P = functools.partial
    col_tiles = range(0, D_MODEL, DENSE_COLS)
    b_merges = [P(b_merge, c) for c in col_tiles]
    b_outs = [P(b_out, c) for c in col_tiles]
    a_head = ([a_norm] + [P(a_qkv, c) for c in range(0, D_ATT, DENSE_COLS)]
              + [P(a_z, c) for c in range(0, D_ATT, DENSE_COLS)])
    a_convs = [P(a_conv, c) for c in range(0, D_CONV, DENSE_COLS)]

    def run(tasks):
        for task in tasks:
            task()

    def interleave(slots, extras):
        order = []
        for t, slot in enumerate(slots):
            order += slot + extras.get(t, [])
        return order

    @pl.when(i == 0)
    def _():
        run(a_head + interleave(attention_slots(), {4: a_convs}))

    @pl.when(jnp.logical_and(i > 0, has_a))
    def _():
        extras = {1: b_outs[:1], 2: b_outs[1:], 4: [b_final], 7: a_convs}
        run(b_merges + a_head + interleave(attention_slots(), extras))

    @pl.when(i == n_blocks)
    def _():
        run(b_merges + b_outs + [b_final])


def _const_spec(shape):
    zeros = (0,) * len(shape)
    return pl.BlockSpec(shape, lambda i: zeros, pipeline_mode=pl.Buffered(1))


def kernel(x, norm_g, w_in, rel_bias, w_att_out, conv_w, conv_b, w_conv_out, w_out, final_norm_g):
    b, s, d = x.shape
    assert d == D_MODEL and s % T_BLOCK == 0 and norm_g.shape[0] == 1
    bf16 = jnp.bfloat16
    n_tok = b * s
    n_blocks = n_tok // T_BLOCK
    rb_padded = jnp.pad(rel_bias[0], ((0, 0), (0, REL_PAD - N_REL)))
    xf = x.reshape(n_tok, d)

    w_in_bf = w_in[0].astype(bf16)
    consts = (
        norm_g[0][None, :], w_in_bf, w_in_bf[:, _C_V:_C_ZA].T, rb_padded,
        w_att_out[0].astype(bf16), conv_w[0], conv_b[0][None, :], w_conv_out[0].astype(bf16),
        w_out[0].astype(bf16), final_norm_g[None, :],
    )
    cur_spec = pl.BlockSpec((T_BLOCK, d), lambda i: (jnp.minimum(i, n_blocks - 1), 0))
    prev_spec = pl.BlockSpec((T_BLOCK, d), lambda i: (jnp.maximum(i - 1, 0), 0))
    out = pl.pallas_call(
        functools.partial(_block_kernel, steps_per_seq=s // T_BLOCK, n_blocks=n_blocks),
        grid=(n_blocks + 1,),
        in_specs=[cur_spec, prev_spec] + [_const_spec(o.shape) for o in consts],
        out_specs=prev_spec,
        out_shape=jax.ShapeDtypeStruct((n_tok, d), jnp.float32),
        scratch_shapes=[
            pltpu.VMEM((ATT_HEADS, WIN, Q_BLOCK), jnp.float32),
            pltpu.VMEM((T_BLOCK, D_MODEL), bf16),
            pltpu.VMEM((ATT_HEADS, T_BLOCK, LANES), bf16),
            pltpu.VMEM((ATT_HEADS, HALO + T_BLOCK, LANES), bf16),
            pltpu.VMEM((ATT_HEADS, LANES, HALO + T_BLOCK), bf16),
            pltpu.VMEM((T_BLOCK, D_ATT), jnp.float32),
            pltpu.VMEM((CARRY_ROWS + T_BLOCK, D_CONV), jnp.float32),
            pltpu.VMEM((T_BLOCK, D_CONV), bf16),
            pltpu.VMEM((T_BLOCK, D_ATT), bf16),
            pltpu.VMEM((T_BLOCK, D_MODEL), bf16),
        ],
        compiler_params=pltpu.CompilerParams(
            dimension_semantics=("arbitrary",), vmem_limit_bytes=VMEM_LIMIT_BYTES),
        name="hybrid_block",
    )(xf, xf, *consts)
    return out.reshape(b, s, d)
```

```python
import functools

import jax
import jax.numpy as jnp
from jax import lax
from jax.experimental import pallas as pl
from jax.experimental.pallas import tpu as pltpu

D_MODEL = 1024
CHUNK = 64
N_LEFT_CHUNKS = 8
ATT_HEADS = 8
ATT_HEAD_DIM = 64
D_ATT = ATT_HEADS * ATT_HEAD_DIM
D_CONV = D_MODEL // 2
CONV_WIDTH = 3
MAX_REL = 128
N_REL = 2 * MAX_REL + 1
EPS = 1e-6
NEG_BIG = -1e30
IN_COLS = 4 * D_ATT + 4 * D_CONV + 2 * D_MODEL
LOG2_E = 1.4426950408889634

LANES = 128
HEADS_PER_LANE_GROUP = LANES // ATT_HEAD_DIM
MXU_TILE = 256
DENSE_COLS = 2 * MXU_TILE
T_BLOCK = 512
Q_BLOCK = 256
HALO = N_LEFT_CHUNKS * CHUNK
WIN = HALO + Q_BLOCK
BAND = HALO + CHUNK
SLAB = -(-(BAND + CHUNK) // LANES) * LANES
CARRY_ROWS = 8
CHUNK_SHIFT = CHUNK.bit_length() - 1
REL_PAD = -(-N_REL // LANES) * LANES
VMEM_LIMIT_BYTES = 58 * 1024 * 1024

assert T_BLOCK == HALO and T_BLOCK % Q_BLOCK == 0 and HEADS_PER_LANE_GROUP == 2

_C_Q, _C_K, _C_V, _C_ZA = 0, D_ATT, 2 * D_ATT, 3 * D_ATT
_C_GB = 4 * D_ATT
_C_GC, _C_U, _C_ZC = _C_GB + D_CONV, _C_GB + 2 * D_CONV, _C_GB + 3 * D_CONV
_C_GA = _C_GB + 4 * D_CONV
_C_GCV = _C_GA + D_MODEL


def _dot(a, b):
    return jnp.dot(a, b, preferred_element_type=jnp.float32)


def _build_bias_table(rb_ref, tbl_ref):
    f32, bf16 = jnp.float32, jnp.bfloat16
    rb = rb_ref[...]
    hi = rb.astype(bf16)
    rem = rb - hi.astype(f32)
    mid = rem.astype(bf16)
    lo = (rem - mid.astype(f32)).astype(bf16)
    src = lax.broadcasted_iota(jnp.int32, (REL_PAD, WIN), 0)
    m = lax.broadcasted_iota(jnp.int32, (REL_PAD, WIN), 1)
    idx = jnp.where(m > HALO + MAX_REL, 2 * MAX_REL, jnp.clip(HALO + MAX_REL - m, 0, 2 * MAX_REL))
    onehot = (src == idx).astype(bf16)
    g = _dot(hi, onehot) + _dot(mid, onehot) + _dot(lo, onehot)
    r = lax.broadcasted_iota(jnp.int32, (Q_BLOCK, WIN), 0)
    c = lax.broadcasted_iota(jnp.int32, (Q_BLOCK, WIN), 1)
    dc = (lax.shift_right_logical(r, CHUNK_SHIFT) + N_LEFT_CHUNKS) - lax.shift_right_logical(c, CHUNK_SHIFT)
    valid = jnp.logical_and(dc >= 0, dc <= N_LEFT_CHUNKS)
    for hd in range(ATT_HEADS):
        rows = jnp.broadcast_to(g[hd:hd + 1, :], (Q_BLOCK, WIN))
        toeplitz = pltpu.roll(rows, 0, 1, stride=1, stride_axis=0)
        tbl_ref[hd] = jnp.where(valid, toeplitz * LOG2_E, NEG_BIG).T


def _block_kernel(x_ref, xp_ref, ng_ref, win_ref, rb_ref, wao_ref, cw_ref, cb_ref, wco_ref, wo_ref, fg_ref,
                  out_ref, tbl_ref, wvt_ref, h_ref, qx_ref, kh_ref, vht_ref, sz_ref, cu_ref, gconv_ref, gatt_ref, m_ref,
                  *, steps_per_seq, n_blocks):
    i = pl.program_id(0)
    has_a = i < n_blocks
    is_start = jnp.logical_and(has_a, (i % steps_per_seq) == 0)
    f32, bf16 = jnp.float32, jnp.bfloat16

    lane = lax.broadcasted_iota(jnp.int32, (1, LANES), 1)
    own_half = [(lane >= j * ATT_HEAD_DIM) & (lane < (j + 1) * ATT_HEAD_DIM) for j in range(2)]
    offset_lane = [lane == (1 - j) * ATT_HEAD_DIM for j in range(2)]
    sub = lax.broadcasted_iota(jnp.int32, (LANES, 1), 0)
    own_rows = [(sub >= j * ATT_HEAD_DIM) & (sub < (j + 1) * ATT_HEAD_DIM) for j in range(2)]

    @pl.when(i == 0)
    def _():
        _build_bias_table(rb_ref, tbl_ref)
        for r0 in range(0, D_MODEL, LANES):
            blk = win_ref[r0:r0 + LANES, _C_V:_C_ZA].astype(f32)
            wvt_ref[:, r0:r0 + LANES] = blk.T.astype(bf16)

    @pl.when(is_start)
    def _():
        for hd in range(ATT_HEADS):
            row = jnp.where(offset_lane[hd % 2], NEG_BIG, 0.0).astype(bf16)
            kh_ref[hd, 0:HALO, :] = jnp.broadcast_to(row, (HALO, LANES))
        vht_ref[:, :, 0:HALO] = jnp.zeros((ATT_HEADS, LANES, HALO), bf16)
        cu_ref[0:CARRY_ROWS, :] = jnp.zeros((CARRY_ROWS, D_CONV), f32)

    @pl.when(jnp.logical_and(has_a, jnp.logical_not(is_start)))
    def _():
        kh_ref[:, 0:HALO, :] = kh_ref[:, T_BLOCK:T_BLOCK + HALO, :]
        vht_ref[:, :, 0:HALO] = vht_ref[:, :, T_BLOCK:T_BLOCK + HALO]

    def proj(col):
        return _dot(h_ref[...], win_ref[:, col:col + DENSE_COLS])

    def a_norm():
        x = x_ref[...]
        r = lax.rsqrt(jnp.mean(x * x, axis=-1, keepdims=True) + EPS)
        h_ref[...] = ((x * r) * ng_ref[...]).astype(bf16)

    def a_qkv(c):
        q2 = (proj(_C_Q + c) * (ATT_HEAD_DIM ** -0.5 * LOG2_E)).astype(bf16)
        k2 = proj(_C_K + c).astype(bf16)
        vt = lax.dot_general(wvt_ref[c:c + DENSE_COLS, :], h_ref[...], (((1,), (1,)), ((), ())),
                             preferred_element_type=f32).astype(bf16)
        for g in range(DENSE_COLS // LANES):
            lanes = slice(g * LANES, (g + 1) * LANES)
            qp, kp, vtp = q2[:, lanes], k2[:, lanes], vt[lanes, :]
            for j in range(2):
                hd = 2 * (c // LANES + g) + j
                qx_ref[hd] = jnp.where(own_half[j], qp, jnp.where(offset_lane[j], 1.0, 0.0).astype(bf16))
                kh_ref[hd, HALO:HALO + T_BLOCK, :] = jnp.where(own_half[j], kp, jnp.zeros_like(kp))
                vht_ref[hd, :, HALO:HALO + T_BLOCK] = jnp.where(own_rows[j], vtp, jnp.ones_like(vtp))

    def a_z(c):
        sz_ref[:, c:c + DENSE_COLS] = jax.nn.silu(proj(_C_ZA + c))

    def a_conv(c):
        cols = slice(c, c + DENSE_COLS)
        cu = proj(_C_GC + c) * proj(_C_U + c)
        cu_ref[CARRY_ROWS:CARRY_ROWS + T_BLOCK, cols] = cu
        cw = cw_ref[:, cols]
        vconv = (cw[0:1, :] * cu_ref[CARRY_ROWS - 2:CARRY_ROWS - 2 + T_BLOCK, cols]
                 + cw[1:2, :] * cu_ref[CARRY_ROWS - 1:CARRY_ROWS - 1 + T_BLOCK, cols]
                 + cw[2:3, :] * cu) + cb_ref[:, cols]
        cu_ref[0:CARRY_ROWS, cols] = cu_ref[T_BLOCK:T_BLOCK + CARRY_ROWS, cols]
        gconv_ref[:, cols] = (proj(_C_GB + c) * vconv * jax.nn.silu(proj(_C_ZC + c))).astype(bf16)

    def scores_t(b, hd):
        q0 = b * Q_BLOCK
        return lax.dot_general(kh_ref[hd, q0:q0 + WIN, :], qx_ref[hd, q0:q0 + Q_BLOCK, :],
                               (((1,), (1,)), ((), ())), preferred_element_type=f32)

    def probabilities_t(st, hd):
        cols = []
        for g in range(Q_BLOCK // LANES):
            lanes = slice(g * LANES, (g + 1) * LANES)
            lo = g * LANES
            sg = st[lo:lo + SLAB, lanes] + tbl_ref[hd, lo:lo + SLAB, lanes]
            eg = jnp.exp2(sg - jnp.max(sg, axis=0, keepdims=True)).astype(bf16)
            pieces = ([jnp.zeros((lo, LANES), bf16)] if lo else []) + [eg]
            if lo + SLAB < WIN:
                pieces.append(jnp.zeros((WIN - lo - SLAB, LANES), bf16))
            cols.append(jnp.concatenate(pieces, axis=0))
        return jnp.concatenate(cols, axis=1)

    def weighted_values_t(et, b, hd):
        q0 = b * Q_BLOCK
        j = hd % 2
        ot = _dot(vht_ref[hd, :, q0:q0 + WIN], et)
        own = ot[j * ATT_HEAD_DIM:(j + 1) * ATT_HEAD_DIM, :]
        sums = ot[(1 - j) * ATT_HEAD_DIM:(2 - j) * ATT_HEAD_DIM, :]
        return own * (1.0 / sums)

    def attention_slots():
        units = [(b, p) for b in range(T_BLOCK // Q_BLOCK) for p in range(ATT_HEADS // 2)]
        state = {}

        def s_phase(n):
            b, p = units[n]
            state[n] = [scores_t(b, hd) for hd in (2 * p, 2 * p + 1)]

        def e_phase(n):
            p = units[n][1]
            state[n] = [probabilities_t(st, hd) for st, hd in zip(state[n], (2 * p, 2 * p + 1))]

        def v_phase(n):
            b, p = units[n]
            rows = slice(b * Q_BLOCK, (b + 1) * Q_BLOCK)
            lanes = slice(p * LANES, (p + 1) * LANES)
            ot = [weighted_values_t(et, b, hd) for et, hd in zip(state.pop(n), (2 * p, 2 * p + 1))]
            att = jnp.concatenate(ot, axis=0).T
            gatt_ref[rows, lanes] = (att * sz_ref[rows, lanes]).astype(bf16)

        phases = (s_phase, e_phase, v_phase)
        return [[functools.partial(ph, t - d) for d, ph in enumerate(phases) if 0 <= t - d < len(units)]
                for t in range(len(units) + len(phases) - 1)]

    def b_merge(c):
        cols = slice(c, c + DENSE_COLS)
        y_att = _dot(gatt_ref[...], wao_ref[:, cols])
        y_conv = _dot(gconv_ref[...], wco_ref[:, cols])
        m = jax.nn.sigmoid(proj(_C_GA + c)) * y_att + jax.nn.sigmoid(proj(_C_GCV + c)) * y_conv
        m_ref[:, cols] = m.astype(bf16)

    def b_out(c):
        cols = slice(c, c + DENSE_COLS)
        out_ref[:, cols] = xp_ref[:, cols] + _dot(m_ref[...], wo_ref[:, cols])

    def b_final():
        xo = out_ref[...]
        r2 = lax.rsqrt(jnp.mean(xo * xo, axis=-1, keepdims=True) + EPS)
        out_ref[...] = (xo * r2) * fg_ref[...]

    P = functools.partial
    col_tiles = range(0, D_MODEL, DENSE_COLS)
    b_merges = [P(b_merge, c) for c in col_tiles]
    b_outs = [P(b_out, c) for c in col_tiles]
    a_head = ([a_norm] + [P(a_qkv, c) for c in range(0, D_ATT, DENSE_COLS)]
              + [P(a_z, c) for c in range(0, D_ATT, DENSE_COLS)])
    a_convs = [P(a_conv, c) for c in range(0, D_CONV, DENSE_COLS)]

    def run(tasks):
        for task in tasks:
            task()

    def interleave(slots, extras):
        order = []
        for t, slot in enumerate(slots):
            order += slot + extras.get(t, [])
        return order

    @pl.when(i == 0)
    def _():
        run(a_head + interleave(attention_slots(), {4: a_convs}))

    @pl.when(jnp.logical_and(i > 0, has_a))
    def _():
        extras = {1: b_outs[:1], 2: b_outs[1:], 4: [b_final], 7: a_convs}
        run(b_merges + a_head + interleave(attention_slots(), extras))

    @pl.when(i == n_blocks)
    def _():
        run(b_merges + b_outs + [b_final])


def _const_spec(shape):
    zeros = (0,) * len(shape)
    return pl.BlockSpec(shape, lambda i: zeros, pipeline_mode=pl.Buffered(1))


def kernel(x, norm_g, w_in, rel_bias, w_att_out, conv_w, conv_b, w_conv_out, w_out, final_norm_g):
    b, s, d = x.shape
    assert d == D_MODEL and s % T_BLOCK == 0 and norm_g.shape[0] == 1
    bf16 = jnp.bfloat16
    n_tok = b * s
    n_blocks = n_tok // T_BLOCK
    rb_padded = jnp.pad(rel_bias[0], ((0, 0), (0, REL_PAD - N_REL)))
    xf = x.reshape(n_tok, d)

    consts = (
        norm_g[0][None, :], w_in[0].astype(bf16), rb_padded,
        w_att_out[0].astype(bf16), conv_w[0], conv_b[0][None, :], w_conv_out[0].astype(bf16),
        w_out[0].astype(bf16), final_norm_g[None, :],
    )
    cur_spec = pl.BlockSpec((T_BLOCK, d), lambda i: (jnp.minimum(i, n_blocks - 1), 0))
    prev_spec = pl.BlockSpec((T_BLOCK, d), lambda i: (jnp.maximum(i - 1, 0), 0))
    out = pl.pallas_call(
        functools.partial(_block_kernel, steps_per_seq=s // T_BLOCK, n_blocks=n_blocks),
        grid=(n_blocks + 1,),
        in_specs=[cur_spec, prev_spec] + [_const_spec(o.shape) for o in consts],
        out_specs=prev_spec,
        out_shape=jax.ShapeDtypeStruct((n_tok, d), jnp.float32),
        scratch_shapes=[
            pltpu.VMEM((ATT_HEADS, WIN, Q_BLOCK), jnp.float32),
            pltpu.VMEM((D_ATT, D_MODEL), bf16),
            pltpu.VMEM((T_BLOCK, D_MODEL), bf16),
            pltpu.VMEM((ATT_HEADS, T_BLOCK, LANES), bf16),
            pltpu.VMEM((ATT_HEADS, HALO + T_BLOCK, LANES), bf16),
            pltpu.VMEM((ATT_HEADS, LANES, HALO + T_BLOCK), bf16),
            pltpu.VMEM((T_BLOCK, D_ATT), jnp.float32),
            pltpu.VMEM((CARRY_ROWS + T_BLOCK, D_CONV), jnp.float32),
            pltpu.VMEM((T_BLOCK, D_CONV), bf16),
            pltpu.VMEM((T_BLOCK, D_ATT), bf16),
            pltpu.VMEM((T_BLOCK, D_MODEL), bf16),
        ],
        compiler_params=pltpu.CompilerParams(
            dimension_semantics=("arbitrary",), vmem_limit_bytes=VMEM_LIMIT_BYTES),
        name="hybrid_block",
    )(xf, xf, *consts)
    return out.reshape(b, s, d)
```

```python
import functools

import jax
import jax.numpy as jnp
from jax import lax
from jax.experimental import pallas as pl
from jax.experimental.pallas import tpu as pltpu

D_MODEL = 1024
CHUNK = 64
N_LEFT_CHUNKS = 8
ATT_HEADS = 8
ATT_HEAD_DIM = 64
D_ATT = ATT_HEADS * ATT_HEAD_DIM
D_CONV = D_MODEL // 2
CONV_WIDTH = 3
MAX_REL = 128
N_REL = 2 * MAX_REL + 1
EPS = 1e-6
NEG_BIG = -1e30
IN_COLS = 4 * D_ATT + 4 * D_CONV + 2 * D_MODEL
LOG2_E = 1.4426950408889634

LANES = 128
HEADS_PER_LANE_GROUP = LANES // ATT_HEAD_DIM
MXU_TILE = 256
DENSE_COLS = 2 * MXU_TILE
T_BLOCK = 512
Q_BLOCK = 256
HALO = N_LEFT_CHUNKS * CHUNK
WIN = HALO + Q_BLOCK
BAND = HALO + CHUNK
SLAB = -(-(BAND + CHUNK) // LANES) * LANES
CARRY_ROWS = 8
STAGE_ROWS = 128
CHUNK_SHIFT = CHUNK.bit_length() - 1
REL_PAD = -(-N_REL // LANES) * LANES
VMEM_LIMIT_BYTES = 58 * 1024 * 1024

assert T_BLOCK == HALO and T_BLOCK % Q_BLOCK == 0 and HEADS_PER_LANE_GROUP == 2

_C_Q, _C_K, _C_V, _C_ZA = 0, D_ATT, 2 * D_ATT, 3 * D_ATT
_C_GB = 4 * D_ATT
_C_GC, _C_U, _C_ZC = _C_GB + D_CONV, _C_GB + 2 * D_CONV, _C_GB + 3 * D_CONV
_C_GA = _C_GB + 4 * D_CONV
_C_GCV = _C_GA + D_MODEL


def _dot(a, b):
    return jnp.dot(a, b, preferred_element_type=jnp.float32)


def _sigmoid(x):
    return 0.5 * jnp.tanh(0.5 * x) + 0.5


def _silu(x):
    half = 0.5 * x
    return half * (jnp.tanh(half) + 1.0)


def _bias_table_tasks(rb_ref, rbp_ref, tbl_ref):
    f32, bf16 = jnp.float32, jnp.bfloat16
    rbp_ref[...] = jnp.zeros(rbp_ref.shape, f32)
    rbp_ref[:, 0:N_REL] = rb_ref[...]
    rb = rbp_ref[...]
    hi = rb.astype(bf16)
    rem = rb - hi.astype(f32)
    mid = rem.astype(bf16)
    lo = (rem - mid.astype(f32)).astype(bf16)
    src = lax.broadcasted_iota(jnp.int32, (REL_PAD, WIN), 0)
    m = lax.broadcasted_iota(jnp.int32, (REL_PAD, WIN), 1)
    idx = jnp.where(m > HALO + MAX_REL, 2 * MAX_REL, jnp.clip(HALO + MAX_REL - m, 0, 2 * MAX_REL))
    onehot = (src == idx).astype(bf16)
    g = _dot(hi, onehot) + _dot(mid, onehot) + _dot(lo, onehot)

    def head_table(hd):
        r = lax.broadcasted_iota(jnp.int32, (Q_BLOCK, WIN), 0)
        c = lax.broadcasted_iota(jnp.int32, (Q_BLOCK, WIN), 1)
        dc = (lax.shift_right_logical(r, CHUNK_SHIFT) + N_LEFT_CHUNKS) - lax.shift_right_logical(c, CHUNK_SHIFT)
        valid = jnp.logical_and(dc >= 0, dc <= N_LEFT_CHUNKS)
        rows = jnp.broadcast_to(g[hd:hd + 1, :], (Q_BLOCK, WIN))
        toeplitz = pltpu.roll(rows, 0, 1, stride=1, stride_axis=0)
        transposed = jnp.where(valid, toeplitz * LOG2_E, NEG_BIG).T
        for grp in range(Q_BLOCK // LANES):
            tbl_ref[hd, grp] = transposed[grp * LANES:grp * LANES + SLAB, grp * LANES:(grp + 1) * LANES]

    return [functools.partial(head_table, hd) for hd in range(ATT_HEADS)]


def _load_weights_as_bf16(pairs, stage_ref, sem_ref, side_tasks):
    jobs = [(src, dst, r0) for src, dst in pairs for r0 in range(0, dst.shape[0], STAGE_ROWS)]

    def chunk_copy(n):
        src, dst, r0 = jobs[n]
        return pltpu.make_async_copy(src.at[0, pl.ds(r0, STAGE_ROWS), :],
                                     stage_ref.at[n % 2, :, pl.ds(0, dst.shape[1])], sem_ref.at[n % 2])

    every = len(jobs) // len(side_tasks)
    chunk_copy(0).start()
    for n, (_, dst, r0) in enumerate(jobs):
        if n + 1 < len(jobs):
            chunk_copy(n + 1).start()
        chunk_copy(n).wait()
        dst[r0:r0 + STAGE_ROWS, :] = stage_ref[n % 2, :, 0:dst.shape[1]].astype(jnp.bfloat16)
        if n % every == every - 1 and n // every < len(side_tasks):
            side_tasks[n // every]()


def _block_kernel(x_ref, xp_ref, ng_ref, win_hbm, rb_ref, wao_hbm, cw_ref, cb_ref, wco_hbm, wo_hbm, fg_ref,
                  out_ref, win_ref, wao_ref, wco_ref, wo_ref, stage_ref, stage_sem, rbp_ref, tbl_ref, wvt_ref, h_ref, qx_ref, kh_ref, vht_ref, sz_ref, cu_ref, gconv_ref, gatt_ref, m_ref,
                  *, steps_per_seq, n_blocks):
    i = pl.program_id(0)
    has_a = i < n_blocks
    is_start = jnp.logical_and(has_a, (i % steps_per_seq) == 0)
    f32, bf16 = jnp.float32, jnp.bfloat16

    lane = lax.broadcasted_iota(jnp.int32, (1, LANES), 1)
    own_half = [(lane >= j * ATT_HEAD_DIM) & (lane < (j + 1) * ATT_HEAD_DIM) for j in range(2)]
    offset_lane = [lane == (1 - j) * ATT_HEAD_DIM for j in range(2)]
    sub = lax.broadcasted_iota(jnp.int32, (LANES, 1), 0)
    own_rows = [(sub >= j * ATT_HEAD_DIM) & (sub < (j + 1) * ATT_HEAD_DIM) for j in range(2)]

    @pl.when(i == 0)
    def _():
        _load_weights_as_bf16(((win_hbm, win_ref), (wao_hbm, wao_ref), (wco_hbm, wco_ref), (wo_hbm, wo_ref)),
                              stage_ref, stage_sem, _bias_table_tasks(rb_ref, rbp_ref, tbl_ref))
        for r0 in range(0, D_MODEL, LANES):
            blk = win_ref[r0:r0 + LANES, _C_V:_C_ZA].astype(f32)
            wvt_ref[:, r0:r0 + LANES] = blk.T.astype(bf16)

    @pl.when(is_start)
    def _():
        for hd in range(ATT_HEADS):
            row = jnp.where(offset_lane[hd % 2], NEG_BIG, 0.0).astype(bf16)
            kh_ref[hd, 0:HALO, :] = jnp.broadcast_to(row, (HALO, LANES))
        vht_ref[:, :, 0:HALO] = jnp.zeros((ATT_HEADS, LANES, HALO), bf16)
        cu_ref[0:CARRY_ROWS, :] = jnp.zeros((CARRY_ROWS, D_CONV), f32)

    @pl.when(jnp.logical_and(has_a, jnp.logical_not(is_start)))
    def _():
        kh_ref[:, 0:HALO, :] = kh_ref[:, T_BLOCK:T_BLOCK + HALO, :]
        vht_ref[:, :, 0:HALO] = vht_ref[:, :, T_BLOCK:T_BLOCK + HALO]

    def proj(col, width=DENSE_COLS):
        return _dot(h_ref[...], win_ref[:, col:col + width])

    def a_norm():
        x = x_ref[...]
        r = lax.rsqrt(jnp.mean(x * x, axis=-1, keepdims=True) + EPS)
        h_ref[...] = ((x * r) * ng_ref[...]).astype(bf16)

    def a_qk(c):
        q2 = (proj(_C_Q + c, MXU_TILE) * (ATT_HEAD_DIM ** -0.5 * LOG2_E)).astype(bf16)
        k2 = proj(_C_K + c, MXU_TILE).astype(bf16)
        for g in range(MXU_TILE // LANES):
            lanes = slice(g * LANES, (g + 1) * LANES)
            qp, kp = q2[:, lanes], k2[:, lanes]
            for j in range(2):
                hd = 2 * (c // LANES + g) + j
                qx_ref[hd] = jnp.where(own_half[j], qp, jnp.where(offset_lane[j], 1.0, 0.0).astype(bf16))
                kh_ref[hd, HALO:HALO + T_BLOCK, :] = jnp.where(own_half[j], kp, jnp.zeros_like(kp))

    def a_v(c):
        vt = lax.dot_general(wvt_ref[c:c + MXU_TILE, :], h_ref[...], (((1,), (1,)), ((), ())),
                             preferred_element_type=f32).astype(bf16)
        for g in range(MXU_TILE // LANES):
            vtp = vt[g * LANES:(g + 1) * LANES, :]
            for j in range(2):
                hd = 2 * (c // LANES + g) + j
                vht_ref[hd, :, HALO:HALO + T_BLOCK] = jnp.where(own_rows[j], vtp, jnp.ones_like(vtp))

    def a_z(c):
        sz_ref[:, c:c + DENSE_COLS] = _silu(proj(_C_ZA + c))

    def conv_proj(r0, col):
        return _dot(h_ref[r0:r0 + Q_BLOCK, :], win_ref[:, col:col + D_CONV])

    def a_conv_in(r0):
        cu_ref[CARRY_ROWS + r0:CARRY_ROWS + r0 + Q_BLOCK, :] = conv_proj(r0, _C_GC) * conv_proj(r0, _C_U)

    def a_conv_gate(r0):
        def tap(t):
            back = CONV_WIDTH - 1 - t
            return cw_ref[t:t + 1, :] * cu_ref[CARRY_ROWS - back + r0:CARRY_ROWS - back + r0 + Q_BLOCK, :]

        vconv = (tap(0) + tap(1) + tap(2)) + cb_ref[...]
        gate = conv_proj(r0, _C_GB) * vconv * _silu(conv_proj(r0, _C_ZC))
        gconv_ref[r0:r0 + Q_BLOCK, :] = gate.astype(bf16)
        if r0 + Q_BLOCK == T_BLOCK:
            cu_ref[0:CARRY_ROWS, :] = cu_ref[T_BLOCK:T_BLOCK + CARRY_ROWS, :]

    def scores_t(b, hd):
        q0 = b * Q_BLOCK
        return lax.dot_general(kh_ref[hd, q0:q0 + WIN, :], qx_ref[hd, q0:q0 + Q_BLOCK, :],
                               (((1,), (1,)), ((), ())), preferred_element_type=f32)

    def probabilities_t(st, hd):
        cols = []
        for g in range(Q_BLOCK // LANES):
            lanes = slice(g * LANES, (g + 1) * LANES)
            lo = g * LANES
            sg = st[lo:lo + SLAB, lanes] + tbl_ref[hd, g]
            eg = jnp.exp2(sg - jnp.max(sg, axis=0, keepdims=True)).astype(bf16)
            pieces = ([jnp.zeros((lo, LANES), bf16)] if lo else []) + [eg]
            if lo + SLAB < WIN:
                pieces.append(jnp.zeros((WIN - lo - SLAB, LANES), bf16))
            cols.append(jnp.concatenate(pieces, axis=0))
        return jnp.concatenate(cols, axis=1)

    def weighted_values_t(et, b, hd):
        q0 = b * Q_BLOCK
        j = hd % 2
        ot = _dot(vht_ref[hd, :, q0:q0 + WIN], et)
        own = ot[j * ATT_HEAD_DIM:(j + 1) * ATT_HEAD_DIM, :]
        sums = ot[(1 - j) * ATT_HEAD_DIM:(2 - j) * ATT_HEAD_DIM, :]
        return own * (1.0 / sums)

    def attention_slots():
        pairs_per_tile = MXU_TILE // LANES
        units = [(b, p0 + p) for p0 in range(0, ATT_HEADS // 2, pairs_per_tile)
                 for b in range(T_BLOCK // Q_BLOCK) for p in range(pairs_per_tile)]
        state = {}

        def s_phase(n):
            b, p = units[n]
            state[n] = [scores_t(b, hd) for hd in (2 * p, 2 * p + 1)]

        def e_phase(n):
            p = units[n][1]
            state[n] = [probabilities_t(st, hd) for st, hd in zip(state[n], (2 * p, 2 * p + 1))]

        def v_phase(n):
            b, p = units[n]
            rows = slice(b * Q_BLOCK, (b + 1) * Q_BLOCK)
            lanes = slice(p * LANES, (p + 1) * LANES)
            ot = [weighted_values_t(et, b, hd) for et, hd in zip(state.pop(n), (2 * p, 2 * p + 1))]
            att = jnp.concatenate(ot, axis=0).T
            gatt_ref[rows, lanes] = (att * sz_ref[rows, lanes]).astype(bf16)

        phases = (s_phase, e_phase, v_phase)
        return [[functools.partial(ph, t - d) for d, ph in enumerate(phases) if 0 <= t - d < len(units)]
                for t in range(len(units) + len(phases) - 1)]

    def b_merge(c):
        cols = slice(c, c + DENSE_COLS)
        y_att = _dot(gatt_ref[...], wao_ref[:, cols])
        y_conv = _dot(gconv_ref[...], wco_ref[:, cols])
        m = _sigmoid(proj(_C_GA + c)) * y_att + _sigmoid(proj(_C_GCV + c)) * y_conv
        m_ref[:, cols] = m.astype(bf16)

    def b_out(r0):
        rows = slice(r0, r0 + Q_BLOCK)
        out_ref[rows, :] = xp_ref[rows, :] + _dot(m_ref[rows, :], wo_ref[...])

    def b_final(r0):
        rows = slice(r0, r0 + Q_BLOCK)
        xo = out_ref[rows, :]
        r2 = lax.rsqrt(jnp.mean(xo * xo, axis=-1, keepdims=True) + EPS)
        out_ref[rows, :] = (xo * r2) * fg_ref[...]

    P = functools.partial
    col_tiles = range(0, D_MODEL, DENSE_COLS)
    b_merges = [P(b_merge, c) for c in col_tiles]
    row_halves = range(0, T_BLOCK, Q_BLOCK)
    b_outs = [P(b_out, r0) for r0 in row_halves]
    b_finals = [P(b_final, r0) for r0 in row_halves]
    a_qks = [P(a_qk, c) for c in range(0, D_ATT, MXU_TILE)]
    a_vs = [P(a_v, c) for c in range(0, D_ATT, MXU_TILE)]
    a_zs = [P(a_z, c) for c in range(0, D_ATT, DENSE_COLS)]
    a_convs = [P(task, r0) for r0 in row_halves for task in (a_conv_in, a_conv_gate)]

    def run(tasks):
        for task in tasks:
            task()

    def interleave(slots, extras):
        order = []
        for t, slot in enumerate(slots):
            order += slot[:1] + extras.get(t, []) + slot[1:]
        return order

    @pl.when(i == 0)
    def _():
        extras = {0: a_qks[1:], 1: a_zs, 2: a_vs[1:], 3: a_convs[0:1], 4: a_convs[1:2], 5: a_convs[2:3],
                  6: a_convs[3:4]}
        run([a_norm] + a_qks[:1] + a_vs[:1] + interleave(attention_slots(), extras))

    @pl.when(jnp.logical_and(i > 0, has_a))
    def _():
        extras = {0: a_qks[1:], 1: a_zs, 2: a_vs[1:], 3: b_outs[:1], 4: b_outs[1:] + b_finals[:1],
                  5: a_convs[0:1] + b_finals[1:], 6: a_convs[1:2], 7: a_convs[2:3], 8: a_convs[3:4]}
        run(b_merges + [a_norm] + a_qks[:1] + a_vs[:1] + interleave(attention_slots(), extras))

    @pl.when(i == n_blocks)
    def _():
        run(b_merges + [task for pair in zip(b_outs, b_finals) for task in pair])


def _const_spec(shape):
    zeros = (0,) * len(shape)
    return pl.BlockSpec(shape, lambda i: zeros, pipeline_mode=pl.Buffered(1))


def kernel(x, norm_g, w_in, rel_bias, w_att_out, conv_w, conv_b, w_conv_out, w_out, final_norm_g):
    b, s, d = x.shape
    assert d == D_MODEL and s % T_BLOCK == 0 and norm_g.shape[0] == 1
    bf16 = jnp.bfloat16
    n_tok = b * s
    n_blocks = n_tok // T_BLOCK
    xf = x.reshape(n_tok, d)

    consts = (norm_g, w_in, rel_bias[0], w_att_out, conv_w[0], conv_b, w_conv_out, w_out, final_norm_g[None, :])
    const_specs = [pl.BlockSpec(memory_space=pl.ANY) if o.ndim == 3 else _const_spec(o.shape) for o in consts]
    cur_spec = pl.BlockSpec((T_BLOCK, d), lambda i: (jnp.minimum(i, n_blocks - 1), 0))
    prev_spec = pl.BlockSpec((T_BLOCK, d), lambda i: (jnp.maximum(i - 1, 0), 0))
    out = pl.pallas_call(
        functools.partial(_block_kernel, steps_per_seq=s // T_BLOCK, n_blocks=n_blocks),
        grid=(n_blocks + 1,),
        in_specs=[cur_spec, prev_spec] + const_specs,
        out_specs=prev_spec,
        out_shape=jax.ShapeDtypeStruct((n_tok, d), jnp.float32),
        scratch_shapes=[
            pltpu.VMEM((D_MODEL, IN_COLS), bf16),
            pltpu.VMEM((D_ATT, D_MODEL), bf16),
            pltpu.VMEM((D_CONV, D_MODEL), bf16),
            pltpu.VMEM((D_MODEL, D_MODEL), bf16),
            pltpu.VMEM((2, STAGE_ROWS, IN_COLS), jnp.float32),
            pltpu.SemaphoreType.DMA((2,)),
            pltpu.VMEM((ATT_HEADS, REL_PAD), jnp.float32),
            pltpu.VMEM((ATT_HEADS, Q_BLOCK // LANES, SLAB, LANES), jnp.float32),
            pltpu.VMEM((D_ATT, D_MODEL), bf16),
            pltpu.VMEM((T_BLOCK, D_MODEL), bf16),
            pltpu.VMEM((ATT_HEADS, T_BLOCK, LANES), bf16),
            pltpu.VMEM((ATT_HEADS, HALO + T_BLOCK, LANES), bf16),
            pltpu.VMEM((ATT_HEADS, LANES, HALO + T_BLOCK), bf16),
            pltpu.VMEM((T_BLOCK, D_ATT), jnp.float32),
            pltpu.VMEM((CARRY_ROWS + T_BLOCK, D_CONV), jnp.float32),
            pltpu.VMEM((T_BLOCK, D_CONV), bf16),
            pltpu.VMEM((T_BLOCK, D_ATT), bf16),
            pltpu.VMEM((T_BLOCK, D_MODEL), bf16),
        ],
        compiler_params=pltpu.CompilerParams(
            dimension_semantics=("arbitrary",), vmem_limit_bytes=VMEM_LIMIT_BYTES),
        name="hybrid_block",
    )(xf, xf, *consts)
    return out.reshape(b, s, d)
```
